```python
import jax, jax.numpy as jnp
from jax import lax
import numpy as np

D_MODEL = 1024
BATCH = 4
SEQ = 4096
DEPTH = 1

EPS = 1e-6
SGU_CHUNK = 128
SGU_GROUPS = 8
SGU_WIDTH = D_MODEL
SGU_GROUP_DIM = SGU_WIDTH // SGU_GROUPS
GLA_HEADS = 4
GLA_DK = D_MODEL // 2
GLA_DV = D_MODEL
GLA_HEAD_DK = GLA_DK // GLA_HEADS
GLA_HEAD_DV = GLA_DV // GLA_HEADS
GLA_LOWRANK = 16
GLA_GATE_NORMALIZER = 16.0
GLA_CHUNK = 64
D_FF = -(-(8 * D_MODEL) // (3 * 256)) * 256
IN_SIZES = (SGU_WIDTH, SGU_WIDTH, GLA_DK, GLA_DK, GLA_DV, GLA_DV, GLA_LOWRANK, D_MODEL, D_MODEL)
IN_COLS = 2 * SGU_WIDTH + 2 * GLA_DK + 2 * GLA_DV + GLA_LOWRANK + 2 * D_MODEL

kernel_name = "hybrid_gmlp_gla_gated_merge"


def rms_norm(x, w):
    xf = x.astype(jnp.float32)
    y = xf * lax.rsqrt(jnp.mean(xf * xf, axis=-1, keepdims=True) + EPS)
    return (y * w.astype(jnp.float32)).astype(x.dtype)


def layer_norm(x, w, b):
    xf = x.astype(jnp.float32)
    mu = jnp.mean(xf, axis=-1, keepdims=True)
    var = jnp.mean(jnp.square(xf - mu), axis=-1, keepdims=True)
    y = (xf - mu) * lax.rsqrt(var + EPS)
    return (y * w.astype(jnp.float32) + b.astype(jnp.float32)).astype(x.dtype)


def chunked_sgu(u, v, ln_w, ln_b, w_s, b_s):
    B, S, _ = v.shape
    n = S // SGU_CHUNK
    v = layer_norm(v, ln_w, ln_b)
    vc = v.reshape(B, n, SGU_CHUNK, SGU_GROUPS, SGU_GROUP_DIM)
    mask = jnp.tril(jnp.ones((SGU_CHUNK, SGU_CHUNK), dtype=bool))
    w = jnp.where(mask[None], w_s, jnp.zeros_like(w_s)).astype(v.dtype)
    mixed = jnp.einsum('gts,bnsgc->bntgc', w, vc) + b_s.T.astype(v.dtype)[:, :, None]
    return u * mixed.reshape(B, S, SGU_WIDTH)


def gla(q, k, v, gk, r, norm_w):
    out_dtype = q.dtype
    B, S, _ = q.shape
    n = S // GLA_CHUNK

    def heads(t, d):
        return t.astype(jnp.float32).reshape(B, n, GLA_CHUNK, GLA_HEADS, d).transpose(0, 3, 1, 2, 4)

    qh = heads(q, GLA_HEAD_DK) * (GLA_HEAD_DK ** -0.5)
    kh = heads(k, GLA_HEAD_DK)
    vh = heads(v, GLA_HEAD_DV)
    g = heads(gk, GLA_HEAD_DK)
    b = jnp.cumsum(g, axis=3)
    b_last = b[:, :, :, -1:, :]
    q_dec = qh * jnp.exp(b)
    k_dec = kh * jnp.exp(-b)
    k_end = kh * jnp.exp(b_last - b)
    mask = jnp.tril(jnp.ones((GLA_CHUNK, GLA_CHUNK), dtype=bool))
    scores = jnp.einsum('bhntd,bhnsd->bhnts', q_dec, k_dec)
    scores = jnp.where(mask, scores, 0.0)
    o_intra = jnp.einsum('bhnts,bhnsv->bhntv', scores, vh)
    kv = jnp.einsum('bhnsd,bhnsv->bhndv', k_end, vh)
    decay = jnp.exp(b_last[:, :, :, 0, :])

    def step(state, inp):
        dec, kv_n = inp
        return dec[..., None] * state + kv_n, state

    s0 = jnp.zeros((B, GLA_HEADS, GLA_HEAD_DK, GLA_HEAD_DV), jnp.float32)
    _, s_prev = lax.scan(step, s0, (jnp.moveaxis(decay, 2, 0), jnp.moveaxis(kv, 2, 0)))
    s_prev = jnp.moveaxis(s_prev, 0, 2)
    o = o_intra + jnp.einsum('bhntd,bhndv->bhntv', q_dec, s_prev)
    o = o * lax.rsqrt(jnp.mean(o * o, axis=-1, keepdims=True) + EPS) * norm_w.astype(jnp.float32)
    o = o.transpose(0, 2, 3, 1, 4).reshape(B, S, GLA_DV)
    return (o * jax.nn.silu(r.astype(jnp.float32))).astype(out_dtype)


def setup_inputs(seed: int = 0) -> dict:
    key = jax.random.key(seed)
    ks = jax.random.split(key, 24)
    L = DEPTH
    nrm = jax.random.normal

    def dense(k, fan_in, fan_out):
        return nrm(k, (L, fan_in, fan_out), jnp.float32) * fan_in ** -0.5

    def gain(k, d):
        return 1.0 + 0.05 * nrm(k, (L, d), jnp.float32)

    return {
        "x": nrm(ks[0], (BATCH, SEQ, D_MODEL), jnp.float32),
        "norm1_w": gain(ks[1], D_MODEL),
        "w_in": dense(ks[2], D_MODEL, IN_COLS),
        "b_in": 0.02 * nrm(ks[3], (L, IN_COLS), jnp.float32),
        "sgu_ln_w": gain(ks[4], SGU_WIDTH),
        "sgu_ln_b": 0.02 * nrm(ks[5], (L, SGU_WIDTH), jnp.float32),
        "sgu_w": 0.5 * SGU_CHUNK ** -0.5 * nrm(ks[6], (L, SGU_GROUPS, SGU_CHUNK, SGU_CHUNK), jnp.float32),
        "sgu_b": 1.0 + 0.1 * nrm(ks[7], (L, SGU_GROUPS, SGU_CHUNK), jnp.float32),
        "w_gk2": dense(ks[8], GLA_LOWRANK, GLA_DK),
        "b_gk": 0.1 * nrm(ks[9], (L, GLA_DK), jnp.float32),
        "gla_norm_w": gain(ks[10], GLA_HEAD_DV),
        "w_branch_a": dense(ks[11], SGU_WIDTH, D_MODEL),
        "w_branch_b": dense(ks[12], GLA_DV, D_MODEL),
        "w_out": dense(ks[13], D_MODEL, D_MODEL),
        "norm2_w": gain(ks[14], D_MODEL),
        "w_ffn_gate": dense(ks[15], D_MODEL, D_FF),
        "w_ffn_up": dense(ks[16], D_MODEL, D_FF),
        "w_ffn_down": dense(ks[17], D_FF, D_MODEL),
        "final_norm_w": 1.0 + 0.05 * nrm(ks[18], (D_MODEL,), jnp.float32),
    }


def reference(x, norm1_w, w_in, b_in, sgu_ln_w, sgu_ln_b, sgu_w, sgu_b, w_gk2, b_gk,
              gla_norm_w, w_branch_a, w_branch_b, w_out, norm2_w, w_ffn_gate, w_ffn_up,
              w_ffn_down, final_norm_w):
    split_points = [int(s) for s in np.cumsum(IN_SIZES)[:-1]]
    h = x
    for l in range(DEPTH):
        xn = rms_norm(h, norm1_w[l])
        proj = xn @ w_in[l] + b_in[l]
        u, v, q, k, vg, r, g_lr, gate_a, gate_b = jnp.split(proj, split_points, axis=-1)
        ya = chunked_sgu(jax.nn.gelu(u), jax.nn.gelu(v), sgu_ln_w[l], sgu_ln_b[l], sgu_w[l], sgu_b[l])
        gk = jax.nn.log_sigmoid((g_lr @ w_gk2[l] + b_gk[l]).astype(jnp.float32)) / GLA_GATE_NORMALIZER
        yb = gla(q, k, vg, gk, r, gla_norm_w[l])
        merged = jax.nn.sigmoid(gate_a) * (ya @ w_branch_a[l]) + jax.nn.sigmoid(gate_b) * (yb @ w_branch_b[l])
        h = h + merged @ w_out[l]
        xn2 = rms_norm(h, norm2_w[l])
        h = h + (jax.nn.silu(xn2 @ w_ffn_gate[l]) * (xn2 @ w_ffn_up[l])) @ w_ffn_down[l]
    return rms_norm(h, final_norm_w)
```

```python
import functools
import math

import jax
import jax.numpy as jnp
from jax import lax
from jax.experimental import pallas as pl
from jax.experimental.pallas import tpu as pltpu

D_MODEL = 1024
EPS = 1e-6
SGU_CHUNK = 128
SGU_GROUPS = 8
SGU_GROUP_DIM = D_MODEL // SGU_GROUPS
GLA_HEADS = 4
GLA_DK = D_MODEL // 2
GLA_DV = D_MODEL
GLA_HEAD_DK = GLA_DK // GLA_HEADS
GLA_HEAD_DV = GLA_DV // GLA_HEADS
GLA_LOWRANK = 16
GLA_GATE_NORMALIZER = 16.0
GLA_CHUNK = 64
LANES = 128
SUBLANES = 8
LOWRANK_PAD = LANES

MIXER_TILE = 256
FFN_TILE = 512
VMEM_LIMIT_BYTES = 56 * 1024 * 1024

F32 = jnp.float32
BF16 = jnp.bfloat16


def _dot(a, b):
    return jnp.dot(a, b, preferred_element_type=F32)


def _gelu_tanh(x):
    c = math.sqrt(2.0 / math.pi)
    return x * (0.5 * (1.0 + jnp.tanh(c * (x + 0.044715 * (x * x * x)))))


def _rms_norm(x, w):
    return x * lax.rsqrt(jnp.mean(x * x, axis=-1, keepdims=True) + EPS) * w


def _mixer_kernel(x_ref, n1w_ref, wuv_ref, buv_ref, wqk_ref, bqk_ref, wvr_ref, bvr_ref,
                  wlr_ref, blr_ref, wgt_ref, bgt_ref, lnw_ref, lnb_ref, sguw_ref, sgub_ref,
                  wgk2_ref, bgk_ref, gnw_ref, wa_ref, wb_ref, wo_ref,
                  h_ref, state_ref, ya_ref, o_ref):
    tile = x_ref.shape[1]
    n_sgu = tile // SGU_CHUNK
    n_gla = tile // GLA_CHUNK

    @pl.when(pl.program_id(1) == 0)
    def _():
        state_ref[...] = jnp.zeros_like(state_ref)

    x = x_ref[0]
    xn = _rms_norm(x, n1w_ref[...]).astype(BF16)

    uv = _dot(xn, wuv_ref[...]) + buv_ref[...]
    gu = _gelu_tanh(uv[:, :D_MODEL])
    gv = _gelu_tanh(uv[:, D_MODEL:])
    mu = jnp.mean(gv, axis=-1, keepdims=True)
    cen = gv - mu
    var = jnp.mean(cen * cen, axis=-1, keepdims=True)
    vln = (cen * lax.rsqrt(var + EPS) * lnw_ref[...] + lnb_ref[...]).astype(BF16)
    row = lax.broadcasted_iota(jnp.int32, (SGU_CHUNK, SGU_CHUNK), 0)
    col = lax.broadcasted_iota(jnp.int32, (SGU_CHUNK, SGU_CHUNK), 1)
    causal = row >= col
    for g in range(SGU_GROUPS):
        cs = slice(g * SGU_GROUP_DIM, (g + 1) * SGU_GROUP_DIM)
        wg = jnp.where(causal, sguw_ref[g], 0.0).astype(BF16)
        rhs = jnp.concatenate(
            [vln[c * SGU_CHUNK:(c + 1) * SGU_CHUNK, cs] for c in range(n_sgu)], axis=1)
        mixed = _dot(wg, rhs)
        for c in range(n_sgu):
            rs = slice(c * SGU_CHUNK, (c + 1) * SGU_CHUNK)
            m = mixed[:, c * SGU_GROUP_DIM:(c + 1) * SGU_GROUP_DIM] + sgub_ref[:, cs]
            ya_ref[rs, cs] = (gu[rs, cs] * m).astype(BF16)

    qk = _dot(xn, wqk_ref[...]) + bqk_ref[...]
    vr = _dot(xn, wvr_ref[...]) + bvr_ref[...]
    glr = (_dot(xn, wlr_ref[...]) + blr_ref[...]).astype(BF16)
    z = _dot(glr, wgk2_ref[...]) + bgk_ref[...]
    gk = (jnp.minimum(z, 0.0) - jnp.log1p(jnp.exp(-jnp.abs(z)))) * (1.0 / GLA_GATE_NORMALIZER)
    pos = lax.broadcasted_iota(jnp.int32, (tile, GLA_DK), 0) % GLA_CHUNK
    b = gk
    shift = 1
    while shift < GLA_CHUNK:
        b = b + jnp.where(pos >= shift, pltpu.roll(b, shift, axis=0), 0.0)
        shift *= 2
    last_rows = [b[c * GLA_CHUNK + GLA_CHUNK - 1:(c + 1) * GLA_CHUNK, :] for c in range(n_gla)]
    b_last = jnp.concatenate(
        [jnp.broadcast_to(r, (GLA_CHUNK, GLA_DK)) for r in last_rows], axis=0)
    q = qk[:, :GLA_DK]
    k = qk[:, GLA_DK:]
    q_dec = (q * (GLA_HEAD_DK ** -0.5) * jnp.exp(b)).astype(BF16)
    k_dec = (k * jnp.exp(-b)).astype(BF16)
    k_end = (k * jnp.exp(b_last - b)).astype(BF16)
    vb = vr[:, :GLA_DV].astype(BF16)
    pad_rows = (-n_gla) % SUBLANES
    last_mat = jnp.concatenate(last_rows + [jnp.zeros((pad_rows, GLA_DK), F32)] * (pad_rows > 0),
                               axis=0)
    decay_t = jnp.exp(last_mat).T
    trow = lax.broadcasted_iota(jnp.int32, (GLA_CHUNK, GLA_CHUNK), 0)
    tcol = lax.broadcasted_iota(jnp.int32, (GLA_CHUNK, GLA_CHUNK), 1)
    tri = trow >= tcol
    for c in range(n_gla):
        rs = slice(c * GLA_CHUNK, (c + 1) * GLA_CHUNK)
        for h in range(GLA_HEADS):
            ks = slice(h * GLA_HEAD_DK, (h + 1) * GLA_HEAD_DK)
            vs = slice(h * GLA_HEAD_DV, (h + 1) * GLA_HEAD_DV)
            qd = q_dec[rs, ks]
            vc = vb[rs, vs]
            scores = lax.dot_general(qd, k_dec[rs, ks], (((1,), (1,)), ((), ())),
                                     preferred_element_type=F32)
            scores = jnp.where(tri, scores, 0.0).astype(BF16)
            s_prev = state_ref[h]
            o_ref[rs, vs] = _dot(scores, vc) + _dot(qd, s_prev.astype(BF16))
            kv = lax.dot_general(k_end[rs, ks], vc, (((0,), (0,)), ((), ())),
                                 preferred_element_type=F32)
            state_ref[h] = decay_t[ks, c:c + 1] * s_prev + kv
    r = vr[:, GLA_DV:]
    rgate = r * jax.nn.sigmoid(r)
    for h in range(GLA_HEADS):
        vs = slice(h * GLA_HEAD_DV, (h + 1) * GLA_HEAD_DV)
        o = o_ref[:, vs]
        on = o * lax.rsqrt(jnp.mean(o * o, axis=-1, keepdims=True) + EPS) * gnw_ref[...]
        o_ref[:, vs] = on * rgate[:, vs]
    yb = o_ref[...].astype(BF16)

    gates = _dot(xn, wgt_ref[...]) + bgt_ref[...]
    pa = _dot(ya_ref[...], wa_ref[...])
    pb = _dot(yb, wb_ref[...])
    merged = jax.nn.sigmoid(gates[:, :D_MODEL]) * pa + jax.nn.sigmoid(gates[:, D_MODEL:]) * pb
    h_ref[0] = x + _dot(merged.astype(BF16), wo_ref[...])


def _ffn_kernel(h_ref, n2w_ref, wg_ref, wu_ref, wd_ref, fnw_ref, out_ref, *, final_norm):
    h = h_ref[...]
    xn = _rms_norm(h, n2w_ref[...]).astype(BF16)
    g = _dot(xn, wg_ref[...])
    u = _dot(xn, wu_ref[...])
    a = (g * jax.nn.sigmoid(g) * u).astype(BF16)
    h2 = h + _dot(a, wd_ref[...])
    if final_norm:
        h2 = _rms_norm(h2, fnw_ref[...])
    out_ref[...] = h2


def _resident(shape):
    nd = len(shape)
    return pl.BlockSpec(shape, lambda *_: (0,) * nd, pipeline_mode=pl.Buffered(1))


def _mixer(x, params):
    batch, seq, _ = x.shape
    tile = MIXER_TILE
    assert seq % tile == 0 and tile % SGU_CHUNK == 0 and tile % GLA_CHUNK == 0
    in_specs = [pl.BlockSpec((1, tile, D_MODEL), lambda b, j: (b, j, 0))]
    in_specs += [_resident(p.shape) for p in params]
    return pl.pallas_call(
        _mixer_kernel,
        grid=(batch, seq // tile),
        in_specs=in_specs,
        out_specs=pl.BlockSpec((1, tile, D_MODEL), lambda b, j: (b, j, 0)),
        out_shape=jax.ShapeDtypeStruct(x.shape, F32),
        scratch_shapes=[
            pltpu.VMEM((GLA_HEADS, GLA_HEAD_DK, GLA_HEAD_DV), F32),
            pltpu.VMEM((tile, D_MODEL), BF16),
            pltpu.VMEM((tile, GLA_DV), F32),
        ],
        compiler_params=pltpu.CompilerParams(
            dimension_semantics=("arbitrary", "arbitrary"),
            vmem_limit_bytes=VMEM_LIMIT_BYTES),
        name="mixer",
    )(x, *params)


def _ffn(h2d, params, final_norm):
    n_tok = h2d.shape[0]
    tile = FFN_TILE
    assert n_tok % tile == 0
    in_specs = [pl.BlockSpec((tile, D_MODEL), lambda i: (i, 0))]
    in_specs += [_resident(p.shape) for p in params]
    return pl.pallas_call(
        functools.partial(_ffn_kernel, final_norm=final_norm),
        grid=(n_tok // tile,),
        in_specs=in_specs,
        out_specs=pl.BlockSpec((tile, D_MODEL), lambda i: (i, 0)),
        out_shape=jax.ShapeDtypeStruct(h2d.shape, F32),
        compiler_params=pltpu.CompilerParams(
            dimension_semantics=("arbitrary",),
            vmem_limit_bytes=VMEM_LIMIT_BYTES),
        name="ffn",
    )(h2d, *params)


def _row(v):
    return v.reshape(1, -1).astype(F32)


def kernel(x, norm1_w, w_in, b_in, sgu_ln_w, sgu_ln_b, sgu_w, sgu_b, w_gk2, b_gk, gla_norm_w,
           w_branch_a, w_branch_b, w_out, norm2_w, w_ffn_gate, w_ffn_up, w_ffn_down,
           final_norm_w):
    depth = w_in.shape[0]
    batch, seq, _ = x.shape
    o_qk = 2 * D_MODEL
    o_vr = o_qk + 2 * GLA_DK
    o_lr = o_vr + 2 * GLA_DV
    o_gt = o_lr + GLA_LOWRANK
    h = x
    for l in range(depth):
        wl, bl = w_in[l], b_in[l]
        lr_pad = LOWRANK_PAD - GLA_LOWRANK
        mixer_params = [
            _row(norm1_w[l]),
            wl[:, :o_qk].astype(BF16), _row(bl[:o_qk]),
            wl[:, o_qk:o_vr].astype(BF16), _row(bl[o_qk:o_vr]),
            wl[:, o_vr:o_lr].astype(BF16), _row(bl[o_vr:o_lr]),
            jnp.pad(wl[:, o_lr:o_gt], ((0, 0), (0, lr_pad))).astype(BF16),
            _row(jnp.pad(bl[o_lr:o_gt], (0, lr_pad))),
            wl[:, o_gt:].astype(BF16), _row(bl[o_gt:]),
            _row(sgu_ln_w[l]), _row(sgu_ln_b[l]),
            sgu_w[l].astype(F32),
            jnp.repeat(sgu_b[l].T.astype(F32), SGU_GROUP_DIM, axis=1),
            jnp.pad(w_gk2[l], ((0, lr_pad), (0, 0))).astype(BF16), _row(b_gk[l]),
            _row(gla_norm_w[l]),
            w_branch_a[l].astype(BF16), w_branch_b[l].astype(BF16), w_out[l].astype(BF16),
        ]
        h = _mixer(h, mixer_params)
        ffn_params = [
            _row(norm2_w[l]),
            w_ffn_gate[l].astype(BF16), w_ffn_up[l].astype(BF16), w_ffn_down[l].astype(BF16),
            _row(final_norm_w),
        ]
        h = _ffn(h.reshape(batch * seq, D_MODEL), ffn_params,
                 final_norm=(l == depth - 1)).reshape(batch, seq, D_MODEL)
    return h
```

```python
import functools
import math

import jax
import jax.numpy as jnp
from jax import lax
from jax.experimental import pallas as pl
from jax.experimental.pallas import tpu as pltpu

D_MODEL = 1024
EPS = 1e-6
SGU_CHUNK = 128
SGU_GROUPS = 8
SGU_GROUP_DIM = D_MODEL // SGU_GROUPS
GLA_HEADS = 4
GLA_DK = D_MODEL // 2
GLA_DV = D_MODEL
GLA_HEAD_DK = GLA_DK // GLA_HEADS
GLA_HEAD_DV = GLA_DV // GLA_HEADS
GLA_LOWRANK = 16
GLA_GATE_NORMALIZER = 16.0
GLA_CHUNK = 64
LANES = 128
SUBLANES = 8
LOWRANK_PAD = LANES

MIXER_TILE = 512
FFN_TILE = 512
VMEM_LIMIT_BYTES = 56 * 1024 * 1024

F32 = jnp.float32
BF16 = jnp.bfloat16


def _dot(a, b):
    return jnp.dot(a, b, preferred_element_type=F32)


def _gelu_tanh(x):
    c = math.sqrt(2.0 / math.pi)
    return x * (0.5 * (1.0 + jnp.tanh(c * (x + 0.044715 * (x * x * x)))))


def _rms_norm(x, w):
    return x * lax.rsqrt(jnp.mean(x * x, axis=-1, keepdims=True) + EPS) * w


def _mixer_kernel(x_ref, n1w_ref, wuv_ref, buv_ref, wqkl_ref, bqkl_ref, wvr_ref, bvr_ref,
                  wgt_ref, bgt_ref, lnw_ref, lnb_ref, sguw_ref, sgub_ref,
                  wgk2_ref, bgk_ref, gnw_ref, wa_ref, wb_ref, wo_ref,
                  h_ref, state_ref):
    tile = x_ref.shape[1]
    n_sgu = tile // SGU_CHUNK
    n_gla = tile // GLA_CHUNK

    @pl.when(pl.program_id(1) == 0)
    def _():
        state_ref[...] = jnp.zeros_like(state_ref)

    x = x_ref[0]
    xn = _rms_norm(x, n1w_ref[...]).astype(BF16)

    qkl = _dot(xn, wqkl_ref[...]) + bqkl_ref[...]
    glr = qkl[:, 2 * GLA_DK:].astype(BF16)
    z = _dot(glr, wgk2_ref[...]) + bgk_ref[...]
    uv = _dot(xn, wuv_ref[:, :2 * D_MODEL]) + buv_ref[...]

    gk = (jnp.minimum(z, 0.0) - jnp.log1p(jnp.exp(-jnp.abs(z)))) * (1.0 / GLA_GATE_NORMALIZER)
    pos = lax.broadcasted_iota(jnp.int32, (tile, GLA_DK), 0) % GLA_CHUNK
    b = gk
    shift = 1
    while shift < GLA_CHUNK:
        b = b + jnp.where(pos >= shift, pltpu.roll(b, shift, axis=0), 0.0)
        shift *= 2
    last_rows = [b[c * GLA_CHUNK + GLA_CHUNK - 1:(c + 1) * GLA_CHUNK, :] for c in range(n_gla)]
    b_last = jnp.concatenate(
        [jnp.broadcast_to(r, (GLA_CHUNK, GLA_DK)) for r in last_rows], axis=0)
    q = qkl[:, :GLA_DK]
    k = qkl[:, GLA_DK:2 * GLA_DK]
    q_dec = (q * (GLA_HEAD_DK ** -0.5) * jnp.exp(b)).astype(BF16)
    k_dec = (k * jnp.exp(-b)).astype(BF16)
    k_end = (k * jnp.exp(b_last - b)).astype(BF16)
    pad_rows = (-n_gla) % SUBLANES
    last_mat = jnp.concatenate(last_rows + [jnp.zeros((pad_rows, GLA_DK), F32)] * (pad_rows > 0),
                               axis=0)
    decay_t = jnp.exp(last_mat).T

    vr = _dot(xn, wvr_ref[:, :2 * GLA_DV]) + bvr_ref[...]

    gu = _gelu_tanh(uv[:, :D_MODEL])
    gv = _gelu_tanh(uv[:, D_MODEL:])
    mu = jnp.mean(gv, axis=-1, keepdims=True)
    cen = gv - mu
    var = jnp.mean(cen * cen, axis=-1, keepdims=True)
    vln = (cen * lax.rsqrt(var + EPS) * lnw_ref[...] + lnb_ref[...]).astype(BF16)

    gates = _dot(xn, wgt_ref[:, :2 * D_MODEL]) + bgt_ref[...]

    row = lax.broadcasted_iota(jnp.int32, (SGU_CHUNK, SGU_CHUNK), 0)
    col = lax.broadcasted_iota(jnp.int32, (SGU_CHUNK, SGU_CHUNK), 1)
    causal = row >= col
    ya_groups = []
    for g in range(SGU_GROUPS):
        cs = slice(g * SGU_GROUP_DIM, (g + 1) * SGU_GROUP_DIM)
        wg = jnp.where(causal, sguw_ref[g], 0.0).astype(BF16)
        rhs = jnp.concatenate(
            [vln[c * SGU_CHUNK:(c + 1) * SGU_CHUNK, cs] for c in range(n_sgu)], axis=1)
        mixed = _dot(wg, rhs)
        mixed_rows = jnp.concatenate(
            [mixed[:, c * SGU_GROUP_DIM:(c + 1) * SGU_GROUP_DIM] + sgub_ref[:, cs]
             for c in range(n_sgu)], axis=0)
        ya_groups.append((gu[:, cs] * mixed_rows).astype(BF16))
    ya = jnp.concatenate(ya_groups, axis=1)

    vb = vr[:, :GLA_DV].astype(BF16)
    trow = lax.broadcasted_iota(jnp.int32, (GLA_CHUNK, GLA_CHUNK), 0)
    tcol = lax.broadcasted_iota(jnp.int32, (GLA_CHUNK, GLA_CHUNK), 1)
    tri = trow >= tcol
    r = vr[:, GLA_DV:]
    rgate = r * jax.nn.sigmoid(r)
    yb_heads = []
    for h in range(GLA_HEADS):
        ks = slice(h * GLA_HEAD_DK, (h + 1) * GLA_HEAD_DK)
        vs = slice(h * GLA_HEAD_DV, (h + 1) * GLA_HEAD_DV)
        state = state_ref[h]
        o_chunks = []
        for c in range(n_gla):
            rs = slice(c * GLA_CHUNK, (c + 1) * GLA_CHUNK)
            qd = q_dec[rs, ks]
            vc = vb[rs, vs]
            scores = lax.dot_general(qd, k_dec[rs, ks], (((1,), (1,)), ((), ())),
                                     preferred_element_type=F32)
            scores = jnp.where(tri, scores, 0.0).astype(BF16)
            o_chunks.append(_dot(scores, vc) + _dot(qd, state.astype(BF16)))
            kv = lax.dot_general(k_end[rs, ks], vc, (((0,), (0,)), ((), ())),
                                 preferred_element_type=F32)
            state = decay_t[ks, c:c + 1] * state + kv
        state_ref[h] = state
        o = jnp.concatenate(o_chunks, axis=0)
        on = o * lax.rsqrt(jnp.mean(o * o, axis=-1, keepdims=True) + EPS) * gnw_ref[...]
        yb_heads.append((on * rgate[:, vs]).astype(BF16))
    yb = jnp.concatenate(yb_heads, axis=1)

    pa = _dot(ya, wa_ref[:, :D_MODEL])
    ga = jax.nn.sigmoid(gates[:, :D_MODEL]) * pa
    pb = _dot(yb, wb_ref[:, :D_MODEL])
    merged = ga + jax.nn.sigmoid(gates[:, D_MODEL:]) * pb
    h_ref[0] = x + _dot(merged.astype(BF16), wo_ref[:, :D_MODEL])


def _ffn_kernel(h_ref, n2w_ref, wg_ref, wu_ref, wd_ref, fnw_ref, out_ref, *, final_norm):
    h = h_ref[...]
    xn = _rms_norm(h, n2w_ref[...]).astype(BF16)
    g = _dot(xn, wg_ref[...])
    u = _dot(xn, wu_ref[...])
    a = (g * jax.nn.sigmoid(g) * u).astype(BF16)
    h2 = h + _dot(a, wd_ref[...])
    if final_norm:
        h2 = _rms_norm(h2, fnw_ref[...])
    out_ref[...] = h2


def _resident(shape):
    nd = len(shape)
    return pl.BlockSpec(shape, lambda *_: (0,) * nd, pipeline_mode=pl.Buffered(1))


def _mixer(x, params):
    batch, seq, _ = x.shape
    tile = MIXER_TILE
    assert seq % tile == 0 and tile % SGU_CHUNK == 0 and tile % GLA_CHUNK == 0
    in_specs = [pl.BlockSpec((1, tile, D_MODEL), lambda b, j: (b, j, 0))]
    in_specs += [_resident(p.shape) for p in params]
    return pl.pallas_call(
        _mixer_kernel,
        grid=(batch, seq // tile),
        in_specs=in_specs,
        out_specs=pl.BlockSpec((1, tile, D_MODEL), lambda b, j: (b, j, 0)),
        out_shape=jax.ShapeDtypeStruct(x.shape, F32),
        scratch_shapes=[
            pltpu.VMEM((GLA_HEADS, GLA_HEAD_DK, GLA_HEAD_DV), F32),
        ],
        compiler_params=pltpu.CompilerParams(
            dimension_semantics=("arbitrary", "arbitrary"),
            vmem_limit_bytes=VMEM_LIMIT_BYTES),
        name="mixer",
    )(x, *params)


def _ffn(h2d, params, final_norm):
    n_tok = h2d.shape[0]
    tile = FFN_TILE
    assert n_tok % tile == 0
    in_specs = [pl.BlockSpec((tile, D_MODEL), lambda i: (i, 0))]
    in_specs += [_resident(p.shape) for p in params]
    return pl.pallas_call(
        functools.partial(_ffn_kernel, final_norm=final_norm),
        grid=(n_tok // tile,),
        in_specs=in_specs,
        out_specs=pl.BlockSpec((tile, D_MODEL), lambda i: (i, 0)),
        out_shape=jax.ShapeDtypeStruct(h2d.shape, F32),
        compiler_params=pltpu.CompilerParams(
            dimension_semantics=("arbitrary",),
            vmem_limit_bytes=VMEM_LIMIT_BYTES),
        name="ffn",
    )(h2d, *params)


def _row(v):
    return v.reshape(1, -1).astype(F32)


def _padw(w):
    return jnp.pad(w, ((0, 0), (0, LANES))).astype(BF16)


def kernel(x, norm1_w, w_in, b_in, sgu_ln_w, sgu_ln_b, sgu_w, sgu_b, w_gk2, b_gk, gla_norm_w,
           w_branch_a, w_branch_b, w_out, norm2_w, w_ffn_gate, w_ffn_up, w_ffn_down,
           final_norm_w):
    depth = w_in.shape[0]
    batch, seq, _ = x.shape
    o_qk = 2 * D_MODEL
    o_vr = o_qk + 2 * GLA_DK
    o_lr = o_vr + 2 * GLA_DV
    o_gt = o_lr + GLA_LOWRANK
    h = x
    for l in range(depth):
        wl, bl = w_in[l], b_in[l]
        lr_pad = LOWRANK_PAD - GLA_LOWRANK
        mixer_params = [
            _row(norm1_w[l]),
            _padw(wl[:, :o_qk]), _row(bl[:o_qk]),
            jnp.concatenate([wl[:, o_qk:o_vr], wl[:, o_lr:o_gt],
                             jnp.zeros((D_MODEL, lr_pad), wl.dtype)], axis=1).astype(BF16),
            _row(jnp.concatenate([bl[o_qk:o_vr], bl[o_lr:o_gt], jnp.zeros((lr_pad,), bl.dtype)])),
            _padw(wl[:, o_vr:o_lr]), _row(bl[o_vr:o_lr]),
            _padw(wl[:, o_gt:]), _row(bl[o_gt:]),
            _row(sgu_ln_w[l]), _row(sgu_ln_b[l]),
            sgu_w[l].astype(F32),
            jnp.repeat(sgu_b[l].T.astype(F32), SGU_GROUP_DIM, axis=1),
            jnp.pad(w_gk2[l], ((0, lr_pad), (0, 0))).astype(BF16), _row(b_gk[l]),
            _row(gla_norm_w[l]),
            _padw(w_branch_a[l]), _padw(w_branch_b[l]), _padw(w_out[l]),
        ]
        h = _mixer(h, mixer_params)
        ffn_params = [
            _row(norm2_w[l]),
            w_ffn_gate[l].astype(BF16), w_ffn_up[l].astype(BF16), w_ffn_down[l].astype(BF16),
            _row(final_norm_w),
        ]
        h = _ffn(h.reshape(batch * seq, D_MODEL), ffn_params,
                 final_norm=(l == depth - 1)).reshape(batch, seq, D_MODEL)
    return h
```

```python
import functools
import math

import jax
import jax.numpy as jnp
from jax import lax
from jax.experimental import pallas as pl
from jax.experimental.pallas import tpu as pltpu

D_MODEL = 1024
EPS = 1e-6
SGU_CHUNK = 128
SGU_GROUPS = 8
SGU_GROUP_DIM = D_MODEL // SGU_GROUPS
GLA_HEADS = 4
GLA_DK = D_MODEL // 2
GLA_DV = D_MODEL
GLA_HEAD_DK = GLA_DK // GLA_HEADS
GLA_HEAD_DV = GLA_DV // GLA_HEADS
GLA_LOWRANK = 16
GLA_GATE_NORMALIZER = 16.0
GLA_CHUNK = 64
LANES = 128
SUBLANES = 8
LOWRANK_PAD = LANES
LOG2E = 1.4426950408889634

MIXER_TILE = 512
FFN_TILE = 512
VMEM_LIMIT_BYTES = 56 * 1024 * 1024

F32 = jnp.float32
BF16 = jnp.bfloat16


def _dot(a, b):
    return jnp.dot(a, b, preferred_element_type=F32)


def _gelu_tanh(x):
    c0 = math.sqrt(2.0 / math.pi)
    c1 = c0 * 0.044715
    t = jnp.tanh(x * (c0 + c1 * (x * x)))
    return x * (0.5 + 0.5 * t)


def _sigmoid(x):
    return 1.0 / (1.0 + jnp.exp2(x * (-LOG2E)))


def _silu(x):
    hx = 0.5 * x
    return hx + hx * jnp.tanh(hx)


def _rms_norm(x, w):
    return x * lax.rsqrt(jnp.mean(x * x, axis=-1, keepdims=True) + EPS) * w


def _mixer_kernel(x_ref, n1w_ref, wuv_ref, buv_ref, wqkl_ref, bqkl_ref, wvr_ref, bvr_ref,
                  wgt_ref, bgt_ref, lnw_ref, lnb_ref, sguw_ref, sgub_ref,
                  wgk2_ref, bgk_ref, gnw_ref, wa_ref, wb_ref, wo_ref,
                  h_ref, state_ref):
    tile = x_ref.shape[1]
    n_sgu = tile // SGU_CHUNK
    n_gla = tile // GLA_CHUNK

    @pl.when(pl.program_id(1) == 0)
    def _():
        state_ref[...] = jnp.zeros_like(state_ref)

    x = x_ref[0]
    xn = _rms_norm(x, n1w_ref[...]).astype(BF16)

    qkl = _dot(xn, wqkl_ref[...])
    bqkl = bqkl_ref[...]
    glr = (qkl[:, 2 * GLA_DK:] + bqkl[:, 2 * GLA_DK:]).astype(BF16)
    z = _dot(glr, wgk2_ref[...]) + bgk_ref[...]
    uv = _dot(xn, wuv_ref[:, :2 * D_MODEL])
    vr = _dot(xn, wvr_ref[:, :2 * GLA_DV])
    gates = _dot(xn, wgt_ref[:, :2 * D_MODEL])

    soft = jnp.log2(1.0 + jnp.exp2(jnp.abs(z) * (-LOG2E)))
    g2 = (jnp.minimum(z, 0.0) * LOG2E - soft) * (1.0 / GLA_GATE_NORMALIZER)
    pos = lax.broadcasted_iota(jnp.int32, (tile, GLA_DK), 0) & (GLA_CHUNK - 1)
    b2 = g2
    shift = 1
    while shift < GLA_CHUNK:
        b2 = b2 + jnp.where(pos >= shift, pltpu.roll(b2, shift, axis=0), 0.0)
        shift *= 2
    last_rows = [b2[c * GLA_CHUNK + GLA_CHUNK - 1:(c + 1) * GLA_CHUNK, :] for c in range(n_gla)]
    b2_last = jnp.concatenate(
        [jnp.broadcast_to(r, (GLA_CHUNK, GLA_DK)) for r in last_rows], axis=0)
    q = qkl[:, :GLA_DK] + bqkl[:, :GLA_DK]
    k = qkl[:, GLA_DK:2 * GLA_DK] + bqkl[:, GLA_DK:2 * GLA_DK]
    q_dec = (q * (GLA_HEAD_DK ** -0.5) * jnp.exp2(b2)).astype(BF16)
    k_dec = (k * jnp.exp2(-b2)).astype(BF16)
    k_end = (k * jnp.exp2(b2_last - b2)).astype(BF16)
    pad_rows = (-n_gla) % SUBLANES
    last_mat = jnp.concatenate(last_rows + [jnp.zeros((pad_rows, GLA_DK), F32)] * (pad_rows > 0),
                               axis=0)
    decay_t = jnp.exp2(last_mat).T

    buv = buv_ref[...]
    gu = _gelu_tanh(uv[:, :D_MODEL] + buv[:, :D_MODEL])
    gv = _gelu_tanh(uv[:, D_MODEL:] + buv[:, D_MODEL:])
    mu = jnp.mean(gv, axis=-1, keepdims=True)
    cen = gv - mu
    var = jnp.mean(cen * cen, axis=-1, keepdims=True)
    vln = (cen * lax.rsqrt(var + EPS) * lnw_ref[...] + lnb_ref[...]).astype(BF16)
    row = lax.broadcasted_iota(jnp.int32, (SGU_CHUNK, SGU_CHUNK), 0)
    col = lax.broadcasted_iota(jnp.int32, (SGU_CHUNK, SGU_CHUNK), 1)
    causal = row >= col
    ya_groups = []
    for g in range(SGU_GROUPS):
        cs = slice(g * SGU_GROUP_DIM, (g + 1) * SGU_GROUP_DIM)
        wg = jnp.where(causal, sguw_ref[g], 0.0).astype(BF16)
        rhs = jnp.concatenate(
            [vln[c * SGU_CHUNK:(c + 1) * SGU_CHUNK, cs] for c in range(n_sgu)], axis=1)
        mixed = _dot(wg, rhs)
        mixed_rows = jnp.concatenate(
            [mixed[:, c * SGU_GROUP_DIM:(c + 1) * SGU_GROUP_DIM] + sgub_ref[:, cs]
             for c in range(n_sgu)], axis=0)
        ya_groups.append((gu[:, cs] * mixed_rows).astype(BF16))
    ya = jnp.concatenate(ya_groups, axis=1)
    pa = _dot(ya, wa_ref[:, :D_MODEL])

    bvr = bvr_ref[...]
    vb = (vr[:, :GLA_DV] + bvr[:, :GLA_DV]).astype(BF16)
    k_pad = jnp.concatenate([k_dec, jnp.zeros((GLA_CHUNK, GLA_DK), BF16)], axis=0)
    trow = lax.broadcasted_iota(jnp.int32, (GLA_CHUNK, 2 * GLA_CHUNK), 0)
    tcol = lax.broadcasted_iota(jnp.int32, (GLA_CHUNK, 2 * GLA_CHUNK), 1)
    tri = trow >= tcol
    zero_v = jnp.zeros((GLA_CHUNK, GLA_HEAD_DV), BF16)
    units = [(h, c) for h in range(GLA_HEADS) for c in range(n_gla)]
    lhs, kv = {}, {}
    for h, c in units:
        ks = slice(h * GLA_HEAD_DK, (h + 1) * GLA_HEAD_DK)
        vs = slice(h * GLA_HEAD_DV, (h + 1) * GLA_HEAD_DV)
        rs = slice(c * GLA_CHUNK, (c + 1) * GLA_CHUNK)
        qd = q_dec[rs, ks]
        scores = lax.dot_general(qd, k_pad[c * GLA_CHUNK:(c + 2) * GLA_CHUNK, ks],
                                 (((1,), (1,)), ((), ())), preferred_element_type=F32)
        scores = jnp.where(tri, scores, 0.0).astype(BF16)
        lhs[h, c] = jnp.concatenate([scores, qd], axis=1)
        kv[h, c] = lax.dot_general(k_end[rs, ks], vb[rs, vs], (((0,), (0,)), ((), ())),
                                   preferred_element_type=F32)
    r = vr[:, GLA_DV:] + bvr[:, GLA_DV:]
    rgate = _silu(r)
    yb_heads = []
    for h in range(GLA_HEADS):
        ks = slice(h * GLA_HEAD_DK, (h + 1) * GLA_HEAD_DK)
        vs = slice(h * GLA_HEAD_DV, (h + 1) * GLA_HEAD_DV)
        state = state_ref[h]
        o_chunks = []
        for c in range(n_gla):
            rs = slice(c * GLA_CHUNK, (c + 1) * GLA_CHUNK)
            rhs = jnp.concatenate([vb[rs, vs], zero_v, state.astype(BF16)], axis=0)
            o_chunks.append(_dot(lhs[h, c], rhs))
            state = decay_t[ks, c:c + 1] * state + kv[h, c]
        state_ref[h] = state
        o = jnp.concatenate(o_chunks, axis=0)
        on = o * lax.rsqrt(jnp.mean(o * o, axis=-1, keepdims=True) + EPS) * gnw_ref[...]
        yb_heads.append((on * rgate[:, vs]).astype(BF16))
    yb = jnp.concatenate(yb_heads, axis=1)

    bgt = bgt_ref[...]
    ga = _sigmoid(gates[:, :D_MODEL] + bgt[:, :D_MODEL]) * pa
    pb = _dot(yb, wb_ref[:, :D_MODEL])
    merged = ga + _sigmoid(gates[:, D_MODEL:] + bgt[:, D_MODEL:]) * pb
    h_ref[0] = x + _dot(merged.astype(BF16), wo_ref[:, :D_MODEL])


def _ffn_kernel(h_ref, n2w_ref, wg_ref, wu_ref, wd_ref, fnw_ref, out_ref, *, final_norm):
    h = h_ref[...]
    xn = _rms_norm(h, n2w_ref[...]).astype(BF16)
    g = _dot(xn, wg_ref[...])
    u = _dot(xn, wu_ref[...])
    a = (g * jax.nn.sigmoid(g) * u).astype(BF16)
    h2 = h + _dot(a, wd_ref[...])
    if final_norm:
        h2 = _rms_norm(h2, fnw_ref[...])
    out_ref[...] = h2


def _resident(shape):
    nd = len(shape)
    return pl.BlockSpec(shape, lambda *_: (0,) * nd, pipeline_mode=pl.Buffered(1))


def _mixer(x, params):
    batch, seq, _ = x.shape
    tile = MIXER_TILE
    assert seq % tile == 0 and tile % SGU_CHUNK == 0 and tile % GLA_CHUNK == 0
    in_specs = [pl.BlockSpec((1, tile, D_MODEL), lambda b, j: (b, j, 0))]
    in_specs += [_resident(p.shape) for p in params]
    return pl.pallas_call(
        _mixer_kernel,
        grid=(batch, seq // tile),
        in_specs=in_specs,
        out_specs=pl.BlockSpec((1, tile, D_MODEL), lambda b, j: (b, j, 0)),
        out_shape=jax.ShapeDtypeStruct(x.shape, F32),
        scratch_shapes=[
            pltpu.VMEM((GLA_HEADS, GLA_HEAD_DK, GLA_HEAD_DV), F32),
        ],
        compiler_params=pltpu.CompilerParams(
            dimension_semantics=("arbitrary", "arbitrary"),
            vmem_limit_bytes=VMEM_LIMIT_BYTES),
        name="mixer",
    )(x, *params)


def _ffn(h2d, params, final_norm):
    n_tok = h2d.shape[0]
    tile = FFN_TILE
    assert n_tok % tile == 0
    in_specs = [pl.BlockSpec((tile, D_MODEL), lambda i: (i, 0))]
    in_specs += [_resident(p.shape) for p in params]
    return pl.pallas_call(
        functools.partial(_ffn_kernel, final_norm=final_norm),
        grid=(n_tok // tile,),
        in_specs=in_specs,
        out_specs=pl.BlockSpec((tile, D_MODEL), lambda i: (i, 0)),
        out_shape=jax.ShapeDtypeStruct(h2d.shape, F32),
        compiler_params=pltpu.CompilerParams(
            dimension_semantics=("arbitrary",),
            vmem_limit_bytes=VMEM_LIMIT_BYTES),
        name="ffn",
    )(h2d, *params)


def _row(v):
    return v.reshape(1, -1).astype(F32)


def _padw(w):
    return jnp.pad(w, ((0, 0), (0, LANES))).astype(BF16)


def kernel(x, norm1_w, w_in, b_in, sgu_ln_w, sgu_ln_b, sgu_w, sgu_b, w_gk2, b_gk, gla_norm_w,
           w_branch_a, w_branch_b, w_out, norm2_w, w_ffn_gate, w_ffn_up, w_ffn_down,
           final_norm_w):
    depth = w_in.shape[0]
    batch, seq, _ = x.shape
    o_qk = 2 * D_MODEL
    o_vr = o_qk + 2 * GLA_DK
    o_lr = o_vr + 2 * GLA_DV
    o_gt = o_lr + GLA_LOWRANK
    h = x
    for l in range(depth):
        wl, bl = w_in[l], b_in[l]
        lr_pad = LOWRANK_PAD - GLA_LOWRANK
        mixer_params = [
            _row(norm1_w[l]),
            _padw(wl[:, :o_qk]), _row(bl[:o_qk]),
            jnp.concatenate([wl[:, o_qk:o_vr], wl[:, o_lr:o_gt],
                             jnp.zeros((D_MODEL, lr_pad), wl.dtype)], axis=1).astype(BF16),
            _row(jnp.concatenate([bl[o_qk:o_vr], bl[o_lr:o_gt], jnp.zeros((lr_pad,), bl.dtype)])),
            _padw(wl[:, o_vr:o_lr]), _row(bl[o_vr:o_lr]),
            _padw(wl[:, o_gt:]), _row(bl[o_gt:]),
            _row(sgu_ln_w[l]), _row(sgu_ln_b[l]),
            sgu_w[l].astype(F32),
            jnp.repeat(sgu_b[l].T.astype(F32), SGU_GROUP_DIM, axis=1),
            jnp.pad(w_gk2[l], ((0, lr_pad), (0, 0))).astype(BF16), _row(b_gk[l]),
            _row(gla_norm_w[l]),
            _padw(w_branch_a[l]), _padw(w_branch_b[l]), _padw(w_out[l]),
        ]
        h = _mixer(h, mixer_params)
        ffn_params = [
            _row(norm2_w[l]),
            w_ffn_gate[l].astype(BF16), w_ffn_up[l].astype(BF16), w_ffn_down[l].astype(BF16),
            _row(final_norm_w),
        ]
        h = _ffn(h.reshape(batch * seq, D_MODEL), ffn_params,
                 final_norm=(l == depth - 1)).reshape(batch, seq, D_MODEL)
    return h
```

```python
import functools
import math

import jax
import jax.numpy as jnp
from jax import lax
from jax.experimental import pallas as pl
from jax.experimental.pallas import tpu as pltpu

D_MODEL = 1024
EPS = 1e-6
SGU_CHUNK = 128
SGU_GROUPS = 8
SGU_GROUP_DIM = D_MODEL // SGU_GROUPS
GLA_HEADS = 4
GLA_DK = D_MODEL // 2
GLA_DV = D_MODEL
GLA_HEAD_DK = GLA_DK // GLA_HEADS
GLA_HEAD_DV = GLA_DV // GLA_HEADS
GLA_LOWRANK = 16
GLA_GATE_NORMALIZER = 16.0
GLA_CHUNK = 64
LANES = 128
SUBLANES = 8
BF16_SUBLANES = 16
LOWRANK_PAD = LANES
LOG2E = 1.4426950408889634
O_QK = 2 * D_MODEL
O_VR = O_QK + 2 * GLA_DK
O_LR = O_VR + 2 * GLA_DV
O_GT = O_LR + GLA_LOWRANK
WPREP_ROWS = 128

MIXER_TILE = 512
FFN_TILE = 512
VMEM_LIMIT_BYTES = 56 * 1024 * 1024

F32 = jnp.float32
BF16 = jnp.bfloat16


def _dot(a, b):
    return jnp.dot(a, b, preferred_element_type=F32)


def _gelu_tanh(x):
    c0 = math.sqrt(2.0 / math.pi)
    c1 = c0 * 0.044715
    t = jnp.tanh(x * (c0 + c1 * (x * x)))
    return x * (0.5 + 0.5 * t)


def _sigmoid(x):
    return 1.0 / (1.0 + jnp.exp2(x * (-LOG2E)))


def _silu(x):
    hx = 0.5 * x
    return hx + hx * jnp.tanh(hx)


def _rms_norm(x, w):
    return x * lax.rsqrt(jnp.mean(x * x, axis=-1, keepdims=True) + EPS) * w


def _mixer_kernel(x_ref, n1w_ref, wuv_ref, buv_ref, wqkl_ref, bqkl_ref, wvr_ref, bvr_ref,
                  wgt_ref, bgt_ref, lnw_ref, lnb_ref, sguw_ref, sgub_ref,
                  wgk2_ref, bgk_ref, gnw_ref, wa_ref, wb_ref, wo_ref,
                  fg_ref, fu_ref, fd_ref,
                  h_ref, fg_out_ref, fu_out_ref, fd_out_ref, state_ref):
    fg_out_ref[...] = fg_ref[...].astype(BF16)
    fu_out_ref[...] = fu_ref[...].astype(BF16)
    fd_out_ref[...] = fd_ref[...].astype(BF16)

    tile = x_ref.shape[1]
    n_sgu = tile // SGU_CHUNK
    n_gla = tile // GLA_CHUNK

    @pl.when(pl.program_id(1) == 0)
    def _():
        state_ref[...] = jnp.zeros_like(state_ref)

    x = x_ref[0]
    xn = _rms_norm(x, n1w_ref[...]).astype(BF16)

    qkl = _dot(xn, wqkl_ref[...])
    bqkl = bqkl_ref[...]
    glr = (qkl[:, 2 * GLA_DK:] + bqkl[:, 2 * GLA_DK:]).astype(BF16)
    z = _dot(glr, wgk2_ref[...]) + bgk_ref[...]
    uv = _dot(xn, wuv_ref[:, :2 * D_MODEL])
    vr = _dot(xn, wvr_ref[:, :2 * GLA_DV])
    gates = _dot(xn, wgt_ref[:, :2 * D_MODEL])

    soft = jnp.log2(1.0 + jnp.exp2(jnp.abs(z) * (-LOG2E)))
    g2 = (jnp.minimum(z, 0.0) * LOG2E - soft) * (1.0 / GLA_GATE_NORMALIZER)
    pos = lax.broadcasted_iota(jnp.int32, (tile, GLA_DK), 0) & (GLA_CHUNK - 1)
    b2 = g2
    shift = 1
    while shift < GLA_CHUNK:
        b2 = b2 + jnp.where(pos >= shift, pltpu.roll(b2, shift, axis=0), 0.0)
        shift *= 2
    last_rows = [b2[c * GLA_CHUNK + GLA_CHUNK - 1:(c + 1) * GLA_CHUNK, :] for c in range(n_gla)]
    b2_last = jnp.concatenate(
        [jnp.broadcast_to(r, (GLA_CHUNK, GLA_DK)) for r in last_rows], axis=0)
    q = qkl[:, :GLA_DK] + bqkl[:, :GLA_DK]
    k = qkl[:, GLA_DK:2 * GLA_DK] + bqkl[:, GLA_DK:2 * GLA_DK]
    q_dec = (q * (GLA_HEAD_DK ** -0.5) * jnp.exp2(b2)).astype(BF16)
    k_dec = (k * jnp.exp2(-b2)).astype(BF16)
    k_end = (k * jnp.exp2(b2_last - b2)).astype(BF16)
    pad_rows = (-n_gla) % SUBLANES
    last_mat = jnp.concatenate(last_rows + [jnp.zeros((pad_rows, GLA_DK), F32)] * (pad_rows > 0),
                               axis=0)
    decay_t = jnp.exp2(last_mat).T

    buv = buv_ref[...]
    gu = _gelu_tanh(uv[:, :D_MODEL] + buv[:, :D_MODEL])
    gv = _gelu_tanh(uv[:, D_MODEL:] + buv[:, D_MODEL:])
    mu = jnp.mean(gv, axis=-1, keepdims=True)
    cen = gv - mu
    var = jnp.mean(cen * cen, axis=-1, keepdims=True)
    vln = (cen * lax.rsqrt(var + EPS) * lnw_ref[...] + lnb_ref[...]).astype(BF16)
    row = lax.broadcasted_iota(jnp.int32, (SGU_CHUNK, SGU_CHUNK), 0)
    col = lax.broadcasted_iota(jnp.int32, (SGU_CHUNK, SGU_CHUNK), 1)
    causal = row >= col
    ya_groups = []
    for g in range(SGU_GROUPS):
        cs = slice(g * SGU_GROUP_DIM, (g + 1) * SGU_GROUP_DIM)
        wg = jnp.where(causal, sguw_ref[g], 0.0).astype(BF16)
        rhs = jnp.concatenate(
            [vln[c * SGU_CHUNK:(c + 1) * SGU_CHUNK, cs] for c in range(n_sgu)], axis=1)
        mixed = _dot(wg, rhs)
        mixed_rows = jnp.concatenate(
            [mixed[:, c * SGU_GROUP_DIM:(c + 1) * SGU_GROUP_DIM] + sgub_ref[:, cs]
             for c in range(n_sgu)], axis=0)
        ya_groups.append((gu[:, cs] * mixed_rows).astype(BF16))
    ya = jnp.concatenate(ya_groups, axis=1)
    pa = _dot(ya, wa_ref[:, :D_MODEL])

    bvr = bvr_ref[...]
    vb = (vr[:, :GLA_DV] + bvr[:, :GLA_DV]).astype(BF16)
    k_pad = jnp.concatenate([k_dec, jnp.zeros((GLA_CHUNK, GLA_DK), BF16)], axis=0)
    trow = lax.broadcasted_iota(jnp.int32, (GLA_CHUNK, 2 * GLA_CHUNK), 0)
    tcol = lax.broadcasted_iota(jnp.int32, (GLA_CHUNK, 2 * GLA_CHUNK), 1)
    tri = trow >= tcol
    zero_v = jnp.zeros((GLA_CHUNK, GLA_HEAD_DV), BF16)
    units = [(h, c) for h in range(GLA_HEADS) for c in range(n_gla)]
    lhs, kv = {}, {}
    for h, c in units:
        ks = slice(h * GLA_HEAD_DK, (h + 1) * GLA_HEAD_DK)
        vs = slice(h * GLA_HEAD_DV, (h + 1) * GLA_HEAD_DV)
        rs = slice(c * GLA_CHUNK, (c + 1) * GLA_CHUNK)
        qd = q_dec[rs, ks]
        scores = lax.dot_general(qd, k_pad[c * GLA_CHUNK:(c + 2) * GLA_CHUNK, ks],
                                 (((1,), (1,)), ((), ())), preferred_element_type=F32)
        scores = jnp.where(tri, scores, 0.0).astype(BF16)
        lhs[h, c] = jnp.concatenate([scores, qd], axis=1)
        kv[h, c] = lax.dot_general(k_end[rs, ks], vb[rs, vs], (((0,), (0,)), ((), ())),
                                   preferred_element_type=F32)
    r = vr[:, GLA_DV:] + bvr[:, GLA_DV:]
    rgate = _silu(r)
    yb_heads = []
    for h in range(GLA_HEADS):
        ks = slice(h * GLA_HEAD_DK, (h + 1) * GLA_HEAD_DK)
        vs = slice(h * GLA_HEAD_DV, (h + 1) * GLA_HEAD_DV)
        state = state_ref[h]
        o_chunks = []
        for c in range(n_gla):
            rs = slice(c * GLA_CHUNK, (c + 1) * GLA_CHUNK)
            rhs = jnp.concatenate([vb[rs, vs], zero_v, state.astype(BF16)], axis=0)
            o_chunks.append(_dot(lhs[h, c], rhs))
            state = decay_t[ks, c:c + 1] * state + kv[h, c]
        state_ref[h] = state
        o = jnp.concatenate(o_chunks, axis=0)
        on = o * lax.rsqrt(jnp.mean(o * o, axis=-1, keepdims=True) + EPS) * gnw_ref[...]
        yb_heads.append((on * rgate[:, vs]).astype(BF16))
    yb = jnp.concatenate(yb_heads, axis=1)

    bgt = bgt_ref[...]
    ga = _sigmoid(gates[:, :D_MODEL] + bgt[:, :D_MODEL]) * pa
    pb = _dot(yb, wb_ref[:, :D_MODEL])
    merged = ga + _sigmoid(gates[:, D_MODEL:] + bgt[:, D_MODEL:]) * pb
    h_ref[0] = x + _dot(merged.astype(BF16), wo_ref[:, :D_MODEL])


def _ffn_kernel(h_ref, n2w_ref, wg_ref, wu_ref, wd_ref, fnw_ref, out_ref, *, final_norm):
    h = h_ref[...]
    xn = _rms_norm(h, n2w_ref[...]).astype(BF16)
    g = _dot(xn, wg_ref[...])
    u = _dot(xn, wu_ref[...])
    a = (g * jax.nn.sigmoid(g) * u).astype(BF16)
    h2 = h + _dot(a, wd_ref[...])
    if final_norm:
        h2 = _rms_norm(h2, fnw_ref[...])
    out_ref[...] = h2


def _resident(shape):
    nd = len(shape)
    return pl.BlockSpec(shape, lambda *_: (0,) * nd, pipeline_mode=pl.Buffered(1))


def _cast_spec(shape, n_outer, n_inner):
    rows, cols = shape
    n_steps = n_outer * n_inner
    n_blocks = max(nb for nb in range(1, n_steps + 1)
                   if n_steps % nb == 0 and rows % (nb * BF16_SUBLANES) == 0)
    per_block = n_steps // n_blocks
    return pl.BlockSpec((rows // n_blocks, cols),
                        lambda b, j: ((b * n_inner + j) // per_block, 0))


def _mixer(x, params, cast_weights):
    batch, seq, _ = x.shape
    tile = MIXER_TILE
    assert seq % tile == 0 and tile % SGU_CHUNK == 0 and tile % GLA_CHUNK == 0
    n_tiles = seq // tile
    cast_specs = [_cast_spec(w.shape, batch, n_tiles) for w in cast_weights]
    in_specs = [pl.BlockSpec((1, tile, D_MODEL), lambda b, j: (b, j, 0))]
    in_specs += [_resident(p.shape) for p in params]
    in_specs += cast_specs
    return pl.pallas_call(
        _mixer_kernel,
        grid=(batch, n_tiles),
        in_specs=in_specs,
        out_specs=[pl.BlockSpec((1, tile, D_MODEL), lambda b, j: (b, j, 0))] + cast_specs,
        out_shape=[jax.ShapeDtypeStruct(x.shape, F32)]
        + [jax.ShapeDtypeStruct(w.shape, BF16) for w in cast_weights],
        scratch_shapes=[
            pltpu.VMEM((GLA_HEADS, GLA_HEAD_DK, GLA_HEAD_DV), F32),
        ],
        compiler_params=pltpu.CompilerParams(
            dimension_semantics=("arbitrary", "arbitrary"),
            vmem_limit_bytes=VMEM_LIMIT_BYTES),
        name="mixer",
    )(x, *params, *cast_weights)


def _ffn(h2d, params, final_norm):
    n_tok = h2d.shape[0]
    tile = FFN_TILE
    assert n_tok % tile == 0
    in_specs = [pl.BlockSpec((tile, D_MODEL), lambda i: (i, 0))]
    in_specs += [_resident(p.shape) for p in params]
    return pl.pallas_call(
        functools.partial(_ffn_kernel, final_norm=final_norm),
        grid=(n_tok // tile,),
        in_specs=in_specs,
        out_specs=pl.BlockSpec((tile, D_MODEL), lambda i: (i, 0)),
        out_shape=jax.ShapeDtypeStruct(h2d.shape, F32),
        compiler_params=pltpu.CompilerParams(
            dimension_semantics=("arbitrary",),
            vmem_limit_bytes=VMEM_LIMIT_BYTES),
        name="ffn",
    )(h2d, *params)


def _wprep_kernel(win_ref, wa_ref, wb_ref, wo_ref,
                  wuv_out, wqkl_out, wvr_out, wgt_out, wa_out, wb_out, wo_out):
    rows = win_ref.shape[0]
    zero_tile = jnp.zeros((rows, LANES), BF16)

    def put(out_ref, block, tail):
        width = block.shape[1]
        out_ref[:, :width] = block.astype(BF16)
        out_ref[:, width:] = tail

    w = win_ref[...]
    lane = lax.broadcasted_iota(jnp.int32, (rows, LANES), 1)
    lowrank = jnp.where(lane < GLA_LOWRANK, w[:, O_LR:O_LR + LANES], 0.0).astype(BF16)
    put(wuv_out, w[:, :O_QK], zero_tile)
    put(wqkl_out, w[:, O_QK:O_VR], lowrank)
    put(wvr_out, w[:, O_VR:O_LR], zero_tile)
    put(wgt_out, w[:, O_GT:O_GT + 2 * D_MODEL], zero_tile)
    put(wa_out, wa_ref[...], zero_tile)
    put(wb_out, wb_ref[...], zero_tile)
    put(wo_out, wo_ref[...], zero_tile)


def _wprep(w_in, w_a, w_b, w_o):
    rows = WPREP_ROWS
    assert D_MODEL % rows == 0
    widths = [O_QK, O_VR - O_QK, O_LR - O_VR, 2 * D_MODEL, D_MODEL, D_MODEL, D_MODEL]
    row_block = lambda cols: pl.BlockSpec((rows, cols), lambda i: (i, 0))
    return pl.pallas_call(
        _wprep_kernel,
        grid=(D_MODEL // rows,),
        in_specs=[row_block(w_in.shape[1])] + [row_block(D_MODEL)] * 3,
        out_specs=[row_block(n + LANES) for n in widths],
        out_shape=[jax.ShapeDtypeStruct((D_MODEL, n + LANES), BF16) for n in widths],
        compiler_params=pltpu.CompilerParams(
            dimension_semantics=("arbitrary",),
            vmem_limit_bytes=VMEM_LIMIT_BYTES),
        name="wprep",
    )(w_in, w_a, w_b, w_o)


def _row(v):
    return v.reshape(1, -1).astype(F32)


def kernel(x, norm1_w, w_in, b_in, sgu_ln_w, sgu_ln_b, sgu_w, sgu_b, w_gk2, b_gk, gla_norm_w,
           w_branch_a, w_branch_b, w_out, norm2_w, w_ffn_gate, w_ffn_up, w_ffn_down,
           final_norm_w):
    depth = w_in.shape[0]
    batch, seq, _ = x.shape
    h = x
    for l in range(depth):
        bl = b_in[l]
        lr_pad = LOWRANK_PAD - GLA_LOWRANK
        wuv, wqkl, wvr, wgt, wa, wb, wo = _wprep(w_in[l], w_branch_a[l], w_branch_b[l], w_out[l])
        mixer_params = [
            _row(norm1_w[l]),
            wuv, _row(bl[:O_QK]),
            wqkl, _row(jnp.concatenate([bl[O_QK:O_VR], bl[O_LR:O_GT],
                                        jnp.zeros((lr_pad,), bl.dtype)])),
            wvr, _row(bl[O_VR:O_LR]),
            wgt, _row(bl[O_GT:]),
            _row(sgu_ln_w[l]), _row(sgu_ln_b[l]),
            sgu_w[l].astype(F32),
            jnp.repeat(sgu_b[l].T.astype(F32), SGU_GROUP_DIM, axis=1),
            jnp.pad(w_gk2[l], ((0, lr_pad), (0, 0))).astype(BF16), _row(b_gk[l]),
            _row(gla_norm_w[l]),
            wa, wb, wo,
        ]
        h, wg, wu, wd = _mixer(h, mixer_params, [w_ffn_gate[l], w_ffn_up[l], w_ffn_down[l]])
        ffn_params = [_row(norm2_w[l]), wg, wu, wd, _row(final_norm_w)]
        h = _ffn(h.reshape(batch * seq, D_MODEL), ffn_params,
                 final_norm=(l == depth - 1)).reshape(batch, seq, D_MODEL)
    return h
```

```python
import functools
import math

import jax
import jax.numpy as jnp
from jax import lax
from jax.experimental import pallas as pl
from jax.experimental.pallas import tpu as pltpu

D_MODEL = 1024
EPS = 1e-6
SGU_CHUNK = 128
SGU_GROUPS = 8
SGU_GROUP_DIM = D_MODEL // SGU_GROUPS
GLA_HEADS = 4
GLA_DK = D_MODEL // 2
GLA_DV = D_MODEL
GLA_HEAD_DK = GLA_DK // GLA_HEADS
GLA_HEAD_DV = GLA_DV // GLA_HEADS
GLA_LOWRANK = 16
GLA_GATE_NORMALIZER = 16.0
GLA_CHUNK = 64
LANES = 128
SUBLANES = 8
BF16_SUBLANES = 16
LOWRANK_PAD = LANES
LOG2E = 1.4426950408889634
O_QK = 2 * D_MODEL
O_VR = O_QK + 2 * GLA_DK
O_LR = O_VR + 2 * GLA_DV
O_GT = O_LR + GLA_LOWRANK
WPREP_ROWS = 128

MIXER_TILE = 512
FFN_TILE = 512
VMEM_LIMIT_BYTES = 56 * 1024 * 1024

F32 = jnp.float32
BF16 = jnp.bfloat16


def _dot(a, b):
    return jnp.dot(a, b, preferred_element_type=F32)


def _gelu_tanh(x):
    c0 = math.sqrt(2.0 / math.pi)
    c1 = c0 * 0.044715
    t = jnp.tanh(x * (c0 + c1 * (x * x)))
    return x * (0.5 + 0.5 * t)


def _sigmoid(x):
    return 1.0 / (1.0 + jnp.exp2(x * (-LOG2E)))


def _silu(x):
    hx = 0.5 * x
    return hx + hx * jnp.tanh(hx)


def _rms_norm(x, w):
    return x * lax.rsqrt(jnp.mean(x * x, axis=-1, keepdims=True) + EPS) * w


def _mixer_kernel(x_ref, n1w_ref, wuv_ref, buv_ref, wqkl_ref, bqkl_ref, wvr_ref, bvr_ref,
                  wgt_ref, bgt_ref, lnw_ref, lnb_ref, sguw_ref, sgub_ref,
                  wgk2_ref, bgk_ref, gnw_ref, wa_ref, wb_ref, wo_ref,
                  fg_ref, fu_ref, fd_ref,
                  h_ref, fg_out_ref, fu_out_ref, fd_out_ref, state_ref):
    fg_out_ref[...] = fg_ref[...].astype(BF16)
    fu_out_ref[...] = fu_ref[...].astype(BF16)
    fd_out_ref[...] = fd_ref[...].astype(BF16)

    tile = x_ref.shape[1]
    n_sgu = tile // SGU_CHUNK
    n_gla = tile // GLA_CHUNK

    @pl.when(pl.program_id(1) == 0)
    def _():
        state_ref[...] = jnp.zeros_like(state_ref)

    x = x_ref[0]
    xn = _rms_norm(x, n1w_ref[...]).astype(BF16)

    qkl = _dot(xn, wqkl_ref[...])
    bqkl = bqkl_ref[...]
    glr = (qkl[:, 2 * GLA_DK:] + bqkl[:, 2 * GLA_DK:]).astype(BF16)
    z = _dot(glr, wgk2_ref[...]) + bgk_ref[...]
    uv = _dot(xn, wuv_ref[:, :2 * D_MODEL])
    vr = _dot(xn, wvr_ref[:, :2 * GLA_DV])
    gates = _dot(xn, wgt_ref[:, :2 * D_MODEL])

    soft = jnp.log2(1.0 + jnp.exp2(jnp.abs(z) * (-LOG2E)))
    g2 = (jnp.minimum(z, 0.0) * LOG2E - soft) * (1.0 / GLA_GATE_NORMALIZER)
    pos = lax.broadcasted_iota(jnp.int32, (tile, GLA_DK), 0) & (GLA_CHUNK - 1)
    b2 = g2
    shift = 1
    while shift < GLA_CHUNK:
        b2 = b2 + jnp.where(pos >= shift, pltpu.roll(b2, shift, axis=0), 0.0)
        shift *= 2
    last_rows = [b2[c * GLA_CHUNK + GLA_CHUNK - 1:(c + 1) * GLA_CHUNK, :] for c in range(n_gla)]
    b2_last = jnp.concatenate(
        [jnp.broadcast_to(r, (GLA_CHUNK, GLA_DK)) for r in last_rows], axis=0)
    q = qkl[:, :GLA_DK] + bqkl[:, :GLA_DK]
    k = qkl[:, GLA_DK:2 * GLA_DK] + bqkl[:, GLA_DK:2 * GLA_DK]
    q_dec = (q * (GLA_HEAD_DK ** -0.5) * jnp.exp2(b2)).astype(BF16)
    k_dec = (k * jnp.exp2(-b2)).astype(BF16)
    k_end = (k * jnp.exp2(b2_last - b2)).astype(BF16)
    pad_rows = (-n_gla) % SUBLANES
    last_mat = jnp.concatenate(last_rows + [jnp.zeros((pad_rows, GLA_DK), F32)] * (pad_rows > 0),
                               axis=0)
    decay_t = jnp.exp2(last_mat).T

    buv = buv_ref[...]
    gu = _gelu_tanh(uv[:, :D_MODEL] + buv[:, :D_MODEL])
    gv = _gelu_tanh(uv[:, D_MODEL:] + buv[:, D_MODEL:])
    mu = jnp.mean(gv, axis=-1, keepdims=True)
    cen = gv - mu
    var = jnp.mean(cen * cen, axis=-1, keepdims=True)
    vln = (cen * lax.rsqrt(var + EPS) * lnw_ref[...] + lnb_ref[...]).astype(BF16)
    row = lax.broadcasted_iota(jnp.int32, (SGU_CHUNK, SGU_CHUNK), 0)
    col = lax.broadcasted_iota(jnp.int32, (SGU_CHUNK, SGU_CHUNK), 1)
    causal = row >= col
    ya_groups = []
    for g in range(SGU_GROUPS):
        cs = slice(g * SGU_GROUP_DIM, (g + 1) * SGU_GROUP_DIM)
        wg = jnp.where(causal, sguw_ref[g], 0.0).astype(BF16)
        rhs = jnp.concatenate(
            [vln[c * SGU_CHUNK:(c + 1) * SGU_CHUNK, cs] for c in range(n_sgu)], axis=1)
        mixed = _dot(wg, rhs)
        mixed_rows = jnp.concatenate(
            [mixed[:, c * SGU_GROUP_DIM:(c + 1) * SGU_GROUP_DIM] + sgub_ref[:, cs]
             for c in range(n_sgu)], axis=0)
        ya_groups.append((gu[:, cs] * mixed_rows).astype(BF16))
    ya = jnp.concatenate(ya_groups, axis=1)
    pa = _dot(ya, wa_ref[:, :D_MODEL])

    bvr = bvr_ref[...]
    vb = (vr[:, :GLA_DV] + bvr[:, :GLA_DV]).astype(BF16)
    k_pad = jnp.concatenate([k_dec, jnp.zeros((GLA_CHUNK, GLA_DK), BF16)], axis=0)
    trow = lax.broadcasted_iota(jnp.int32, (GLA_CHUNK, 2 * GLA_CHUNK), 0)
    tcol = lax.broadcasted_iota(jnp.int32, (GLA_CHUNK, 2 * GLA_CHUNK), 1)
    tri = trow >= tcol
    zero_v = jnp.zeros((GLA_CHUNK, GLA_HEAD_DV), BF16)
    units = [(h, c) for h in range(GLA_HEADS) for c in range(n_gla)]
    lhs, kv = {}, {}
    for h, c in units:
        ks = slice(h * GLA_HEAD_DK, (h + 1) * GLA_HEAD_DK)
        vs = slice(h * GLA_HEAD_DV, (h + 1) * GLA_HEAD_DV)
        rs = slice(c * GLA_CHUNK, (c + 1) * GLA_CHUNK)
        qd = q_dec[rs, ks]
        scores = lax.dot_general(qd, k_pad[c * GLA_CHUNK:(c + 2) * GLA_CHUNK, ks],
                                 (((1,), (1,)), ((), ())), preferred_element_type=F32)
        scores = jnp.where(tri, scores, 0.0).astype(BF16)
        lhs[h, c] = jnp.concatenate([scores, qd], axis=1)
        kv[h, c] = lax.dot_general(k_end[rs, ks], vb[rs, vs], (((0,), (0,)), ((), ())),
                                   preferred_element_type=F32)
    r = vr[:, GLA_DV:] + bvr[:, GLA_DV:]
    rgate = _silu(r)
    yb_heads = []
    for h in range(GLA_HEADS):
        ks = slice(h * GLA_HEAD_DK, (h + 1) * GLA_HEAD_DK)
        vs = slice(h * GLA_HEAD_DV, (h + 1) * GLA_HEAD_DV)
        state = state_ref[h]
        o_chunks = []
        for c in range(n_gla):
            rs = slice(c * GLA_CHUNK, (c + 1) * GLA_CHUNK)
            rhs = jnp.concatenate([vb[rs, vs], zero_v, state.astype(BF16)], axis=0)
            o_chunks.append(_dot(lhs[h, c], rhs))
            state = decay_t[ks, c:c + 1] * state + kv[h, c]
        state_ref[h] = state
        o = jnp.concatenate(o_chunks, axis=0)
        on = o * lax.rsqrt(jnp.mean(o * o, axis=-1, keepdims=True) + EPS) * gnw_ref[...]
        yb_heads.append((on * rgate[:, vs]).astype(BF16))
    yb = jnp.concatenate(yb_heads, axis=1)

    bgt = bgt_ref[...]
    ga = _sigmoid(gates[:, :D_MODEL] + bgt[:, :D_MODEL]) * pa
    pb = _dot(yb, wb_ref[:, :D_MODEL])
    merged = ga + _sigmoid(gates[:, D_MODEL:] + bgt[:, D_MODEL:]) * pb
    h_ref[0] = x + _dot(merged.astype(BF16), wo_ref[:, :D_MODEL])


def _ffn_kernel(h_ref, n2w_ref, wg_ref, wu_ref, wd_ref, fnw_ref, out_ref, *, final_norm):
    h = h_ref[...]
    xn = _rms_norm(h, n2w_ref[...]).astype(BF16)
    g = _dot(xn, wg_ref[...])
    u = _dot(xn, wu_ref[...])
    a = (g * jax.nn.sigmoid(g) * u).astype(BF16)
    h2 = h + _dot(a, wd_ref[...])
    if final_norm:
        h2 = _rms_norm(h2, fnw_ref[...])
    out_ref[...] = h2


def _resident(shape):
    nd = len(shape)
    return pl.BlockSpec(shape, lambda *_: (0,) * nd, pipeline_mode=pl.Buffered(1))


def _cast_spec(shape, n_outer, n_inner):
    rows, cols = shape
    n_steps = n_outer * n_inner
    n_blocks = max(nb for nb in range(1, n_steps + 1)
                   if n_steps % nb == 0 and rows % (nb * BF16_SUBLANES) == 0)
    per_block = n_steps // n_blocks
    return pl.BlockSpec((rows // n_blocks, cols),
                        lambda b, j: ((b * n_inner + j) // per_block, 0))


def _mixer(x, params, cast_weights):
    batch, seq, _ = x.shape
    tile = MIXER_TILE
    assert seq % tile == 0 and tile % SGU_CHUNK == 0 and tile % GLA_CHUNK == 0
    n_tiles = seq // tile
    cast_specs = [_cast_spec(w.shape, batch, n_tiles) for w in cast_weights]
    in_specs = [pl.BlockSpec((1, tile, D_MODEL), lambda b, j: (b, j, 0))]
    in_specs += [_resident(p.shape) for p in params]
    in_specs += cast_specs
    return pl.pallas_call(
        _mixer_kernel,
        grid=(batch, n_tiles),
        in_specs=in_specs,
        out_specs=[pl.BlockSpec((1, tile, D_MODEL), lambda b, j: (b, j, 0))] + cast_specs,
        out_shape=[jax.ShapeDtypeStruct(x.shape, F32)]
        + [jax.ShapeDtypeStruct(w.shape, BF16) for w in cast_weights],
        scratch_shapes=[
            pltpu.VMEM((GLA_HEADS, GLA_HEAD_DK, GLA_HEAD_DV), F32),
        ],
        compiler_params=pltpu.CompilerParams(
            dimension_semantics=("arbitrary", "arbitrary"),
            vmem_limit_bytes=VMEM_LIMIT_BYTES),
        name="mixer",
    )(x, *params, *cast_weights)


def _ffn(h2d, params, final_norm):
    n_tok = h2d.shape[0]
    tile = FFN_TILE
    assert n_tok % tile == 0
    in_specs = [pl.BlockSpec((tile, D_MODEL), lambda i: (i, 0))]
    in_specs += [_resident(p.shape) for p in params]
    return pl.pallas_call(
        functools.partial(_ffn_kernel, final_norm=final_norm),
        grid=(n_tok // tile,),
        in_specs=in_specs,
        out_specs=pl.BlockSpec((tile, D_MODEL), lambda i: (i, 0)),
        out_shape=jax.ShapeDtypeStruct(h2d.shape, F32),
        compiler_params=pltpu.CompilerParams(
            dimension_semantics=("arbitrary",),
            vmem_limit_bytes=VMEM_LIMIT_BYTES),
        name="ffn",
    )(h2d, *params)


def _wprep_kernel(uv_ref, qk_ref, vr0_ref, vr1_ref, g0_ref, g1_ref, g2_ref, wa_ref, wb_ref, wo_ref,
                  wuv_out, wqkl_out, wvr_out, wgt_out, wa_out, wb_out, wo_out):
    kb = uv_ref.shape[1]
    eye = (lax.broadcasted_iota(jnp.int32, (kb, kb), 0)
           == lax.broadcasted_iota(jnp.int32, (kb, kb), 1)).astype(BF16)

    def transposed(block):
        return lax.dot_general(eye, block.astype(BF16), (((1,), (1,)), ((), ())),
                               preferred_element_type=F32).astype(BF16)

    zero_tile = jnp.zeros((kb, LANES), BF16)

    def put(out_ref, body, tail):
        width = body.shape[1]
        out_ref[:, :width] = body
        out_ref[:, width:] = tail

    g0 = g0_ref[...]
    lane = lax.broadcasted_iota(jnp.int32, (kb, LANES), 1)
    lowrank = jnp.where(lane < GLA_LOWRANK, transposed(g0[:LANES]), 0)
    gate_rows = jnp.concatenate([g0[GLA_LOWRANK:], g1_ref[...], g2_ref[...]], axis=0)
    put(wuv_out, transposed(uv_ref[...]), zero_tile)
    put(wqkl_out, transposed(qk_ref[...]), lowrank)
    put(wvr_out, transposed(jnp.concatenate([vr0_ref[...], vr1_ref[...]], axis=0)), zero_tile)
    put(wgt_out, transposed(gate_rows), zero_tile)
    put(wa_out, wa_ref[...].astype(BF16), zero_tile)
    put(wb_out, wb_ref[...].astype(BF16), zero_tile)
    put(wo_out, wo_ref[...].astype(BF16), zero_tile)


def _wprep(w_in_t, w_a, w_b, w_o):
    kb = WPREP_ROWS
    assert D_MODEL % kb == 0
    half = D_MODEL
    assert O_QK == 2 * half and O_VR == 3 * half and O_LR == 5 * half
    tail = w_in_t.shape[0] - 7 * half
    assert tail == GLA_LOWRANK and (7 * half) % tail == 0

    def win(rows, block_index):
        return pl.BlockSpec((rows, kb), lambda i: (block_index, i))

    widths = [O_QK, O_VR - O_QK, O_LR - O_VR, 2 * D_MODEL, D_MODEL, D_MODEL, D_MODEL]
    row_block = lambda cols: pl.BlockSpec((kb, cols), lambda i: (i, 0))
    return pl.pallas_call(
        _wprep_kernel,
        grid=(D_MODEL // kb,),
        in_specs=[win(2 * half, 0), win(half, 2), win(half, 3), win(half, 4),
                  win(half, 5), win(half, 6), win(tail, 7 * half // tail)]
        + [row_block(D_MODEL)] * 3,
        out_specs=[row_block(n + LANES) for n in widths],
        out_shape=[jax.ShapeDtypeStruct((D_MODEL, n + LANES), BF16) for n in widths],
        compiler_params=pltpu.CompilerParams(
            dimension_semantics=("arbitrary",),
            vmem_limit_bytes=VMEM_LIMIT_BYTES),
        name="wprep",
    )(*([w_in_t] * 7), w_a, w_b, w_o)


def _row(v):
    return v.reshape(1, -1).astype(F32)


def kernel(x, norm1_w, w_in, b_in, sgu_ln_w, sgu_ln_b, sgu_w, sgu_b, w_gk2, b_gk, gla_norm_w,
           w_branch_a, w_branch_b, w_out, norm2_w, w_ffn_gate, w_ffn_up, w_ffn_down,
           final_norm_w):
    depth = w_in.shape[0]
    batch, seq, _ = x.shape
    h = x
    for l in range(depth):
        bl = b_in[l]
        lr_pad = LOWRANK_PAD - GLA_LOWRANK
        wuv, wqkl, wvr, wgt, wa, wb, wo = _wprep(w_in[l].T, w_branch_a[l], w_branch_b[l], w_out[l])
        mixer_params = [
            _row(norm1_w[l]),
            wuv, _row(bl[:O_QK]),
            wqkl, _row(jnp.concatenate([bl[O_QK:O_VR], bl[O_LR:O_GT],
                                        jnp.zeros((lr_pad,), bl.dtype)])),
            wvr, _row(bl[O_VR:O_LR]),
            wgt, _row(bl[O_GT:]),
            _row(sgu_ln_w[l]), _row(sgu_ln_b[l]),
            sgu_w[l].astype(F32),
            jnp.repeat(sgu_b[l].T.astype(F32), SGU_GROUP_DIM, axis=1),
            jnp.pad(w_gk2[l], ((0, lr_pad), (0, 0))).astype(BF16), _row(b_gk[l]),
            _row(gla_norm_w[l]),
            wa, wb, wo,
        ]
        h, wg, wu, wd = _mixer(h, mixer_params, [w_ffn_gate[l], w_ffn_up[l], w_ffn_down[l]])
        ffn_params = [_row(norm2_w[l]), wg, wu, wd, _row(final_norm_w)]
        h = _ffn(h.reshape(batch * seq, D_MODEL), ffn_params,
                 final_norm=(l == depth - 1)).reshape(batch, seq, D_MODEL)
    return h
```

```python
import functools
import math

import jax
import jax.numpy as jnp
from jax import lax
from jax.experimental import pallas as pl
from jax.experimental.pallas import tpu as pltpu

D_MODEL = 1024
EPS = 1e-6
SGU_CHUNK = 128
SGU_GROUPS = 8
SGU_GROUP_DIM = D_MODEL // SGU_GROUPS
GLA_HEADS = 4
GLA_DK = D_MODEL // 2
GLA_DV = D_MODEL
GLA_HEAD_DK = GLA_DK // GLA_HEADS
GLA_HEAD_DV = GLA_DV // GLA_HEADS
GLA_LOWRANK = 16
GLA_GATE_NORMALIZER = 16.0
GLA_CHUNK = 64
LANES = 128
SUBLANES = 8
BF16_SUBLANES = 16
LOWRANK_PAD = LANES
LOG2E = 1.4426950408889634
O_QK = 2 * D_MODEL
O_VR = O_QK + 2 * GLA_DK
O_LR = O_VR + 2 * GLA_DV
O_GT = O_LR + GLA_LOWRANK
WPREP_ROWS = 128

MIXER_TILE = 512
FFN_TILE = 1024
FFN_SUBTILES = 4
VMEM_LIMIT_BYTES = 56 * 1024 * 1024

F32 = jnp.float32
BF16 = jnp.bfloat16


def _dot(a, b):
    return jnp.dot(a, b, preferred_element_type=F32)


def _gelu_tanh(x):
    c0 = math.sqrt(2.0 / math.pi)
    c1 = c0 * 0.044715
    t = jnp.tanh(x * (c0 + c1 * (x * x)))
    return x * (0.5 + 0.5 * t)


def _sigmoid(x):
    return 1.0 / (1.0 + jnp.exp2(x * (-LOG2E)))


def _silu(x):
    hx = 0.5 * x
    return hx + hx * jnp.tanh(hx)


def _rms_norm(x, w):
    return x * lax.rsqrt(jnp.mean(x * x, axis=-1, keepdims=True) + EPS) * w


def _mixer_kernel(x_ref, n1w_ref, wuv_ref, buv_ref, wqkl_ref, bqkl_ref, wvr_ref, bvr_ref,
                  wgt_ref, bgt_ref, lnw_ref, lnb_ref, sguw_ref, sgub_ref,
                  wgk2_ref, bgk_ref, gnw_ref, wa_ref, wb_ref, wo_ref,
                  fg_ref, fu_ref, fd_ref,
                  h_ref, fg_out_ref, fu_out_ref, fd_out_ref, state_ref):
    fg_out_ref[...] = fg_ref[...].astype(BF16)
    fu_out_ref[...] = fu_ref[...].astype(BF16)
    fd_out_ref[...] = fd_ref[...].astype(BF16)

    tile = x_ref.shape[1]
    n_sgu = tile // SGU_CHUNK
    n_gla = tile // GLA_CHUNK

    @pl.when(pl.program_id(1) == 0)
    def _():
        state_ref[...] = jnp.zeros_like(state_ref)

    x = x_ref[0]
    xn = _rms_norm(x, n1w_ref[...]).astype(BF16)

    qkl = _dot(xn, wqkl_ref[...])
    bqkl = bqkl_ref[...]
    glr = (qkl[:, 2 * GLA_DK:] + bqkl[:, 2 * GLA_DK:]).astype(BF16)
    z = _dot(glr, wgk2_ref[...]) + bgk_ref[...]
    uv = _dot(xn, wuv_ref[:, :2 * D_MODEL])
    vr = _dot(xn, wvr_ref[:, :2 * GLA_DV])
    gates = _dot(xn, wgt_ref[:, :2 * D_MODEL])

    soft = jnp.log2(1.0 + jnp.exp2(jnp.abs(z) * (-LOG2E)))
    g2 = (jnp.minimum(z, 0.0) * LOG2E - soft) * (1.0 / GLA_GATE_NORMALIZER)
    pos = lax.broadcasted_iota(jnp.int32, (tile, GLA_DK), 0) & (GLA_CHUNK - 1)
    b2 = g2
    shift = 1
    while shift < GLA_CHUNK:
        b2 = b2 + jnp.where(pos >= shift, pltpu.roll(b2, shift, axis=0), 0.0)
        shift *= 2
    last_rows = [b2[c * GLA_CHUNK + GLA_CHUNK - 1:(c + 1) * GLA_CHUNK, :] for c in range(n_gla)]
    b2_last = jnp.concatenate(
        [jnp.broadcast_to(r, (GLA_CHUNK, GLA_DK)) for r in last_rows], axis=0)
    q = qkl[:, :GLA_DK] + bqkl[:, :GLA_DK]
    k = qkl[:, GLA_DK:2 * GLA_DK] + bqkl[:, GLA_DK:2 * GLA_DK]
    q_dec = (q * (GLA_HEAD_DK ** -0.5) * jnp.exp2(b2)).astype(BF16)
    k_dec = (k * jnp.exp2(-b2)).astype(BF16)
    k_end = (k * jnp.exp2(b2_last - b2)).astype(BF16)
    pad_rows = (-n_gla) % SUBLANES
    last_mat = jnp.concatenate(last_rows + [jnp.zeros((pad_rows, GLA_DK), F32)] * (pad_rows > 0),
                               axis=0)
    decay_t = jnp.exp2(last_mat).T

    buv = buv_ref[...]
    gu = _gelu_tanh(uv[:, :D_MODEL] + buv[:, :D_MODEL])
    gv = _gelu_tanh(uv[:, D_MODEL:] + buv[:, D_MODEL:])
    mu = jnp.mean(gv, axis=-1, keepdims=True)
    cen = gv - mu
    var = jnp.mean(cen * cen, axis=-1, keepdims=True)
    vln = (cen * lax.rsqrt(var + EPS) * lnw_ref[...] + lnb_ref[...]).astype(BF16)
    row = lax.broadcasted_iota(jnp.int32, (SGU_CHUNK, SGU_CHUNK), 0)
    col = lax.broadcasted_iota(jnp.int32, (SGU_CHUNK, SGU_CHUNK), 1)
    causal = row >= col
    ya_groups = []
    for g in range(SGU_GROUPS):
        cs = slice(g * SGU_GROUP_DIM, (g + 1) * SGU_GROUP_DIM)
        wg = jnp.where(causal, sguw_ref[g], 0.0).astype(BF16)
        rhs = jnp.concatenate(
            [vln[c * SGU_CHUNK:(c + 1) * SGU_CHUNK, cs] for c in range(n_sgu)], axis=1)
        mixed = _dot(wg, rhs)
        mixed_rows = jnp.concatenate(
            [mixed[:, c * SGU_GROUP_DIM:(c + 1) * SGU_GROUP_DIM] + sgub_ref[:, cs]
             for c in range(n_sgu)], axis=0)
        ya_groups.append((gu[:, cs] * mixed_rows).astype(BF16))
    ya = jnp.concatenate(ya_groups, axis=1)
    pa = _dot(ya, wa_ref[:, :D_MODEL])

    bvr = bvr_ref[...]
    vb = (vr[:, :GLA_DV] + bvr[:, :GLA_DV]).astype(BF16)
    k_pad = jnp.concatenate([k_dec, jnp.zeros((GLA_CHUNK, GLA_DK), BF16)], axis=0)
    trow = lax.broadcasted_iota(jnp.int32, (GLA_CHUNK, 2 * GLA_CHUNK), 0)
    tcol = lax.broadcasted_iota(jnp.int32, (GLA_CHUNK, 2 * GLA_CHUNK), 1)
    tri = trow >= tcol
    zero_v = jnp.zeros((GLA_CHUNK, GLA_HEAD_DV), BF16)
    units = [(h, c) for h in range(GLA_HEADS) for c in range(n_gla)]
    lhs, kv = {}, {}
    for h, c in units:
        ks = slice(h * GLA_HEAD_DK, (h + 1) * GLA_HEAD_DK)
        vs = slice(h * GLA_HEAD_DV, (h + 1) * GLA_HEAD_DV)
        rs = slice(c * GLA_CHUNK, (c + 1) * GLA_CHUNK)
        qd = q_dec[rs, ks]
        scores = lax.dot_general(qd, k_pad[c * GLA_CHUNK:(c + 2) * GLA_CHUNK, ks],
                                 (((1,), (1,)), ((), ())), preferred_element_type=F32)
        scores = jnp.where(tri, scores, 0.0).astype(BF16)
        lhs[h, c] = jnp.concatenate([scores, qd], axis=1)
        kv[h, c] = lax.dot_general(k_end[rs, ks], vb[rs, vs], (((0,), (0,)), ((), ())),
                                   preferred_element_type=F32)
    r = vr[:, GLA_DV:] + bvr[:, GLA_DV:]
    rgate = _silu(r)
    yb_heads = []
    for h in range(GLA_HEADS):
        ks = slice(h * GLA_HEAD_DK, (h + 1) * GLA_HEAD_DK)
        vs = slice(h * GLA_HEAD_DV, (h + 1) * GLA_HEAD_DV)
        state = state_ref[h]
        o_chunks = []
        for c in range(n_gla):
            rs = slice(c * GLA_CHUNK, (c + 1) * GLA_CHUNK)
            rhs = jnp.concatenate([vb[rs, vs], zero_v, state.astype(BF16)], axis=0)
            o_chunks.append(_dot(lhs[h, c], rhs))
            state = decay_t[ks, c:c + 1] * state + kv[h, c]
        state_ref[h] = state
        o = jnp.concatenate(o_chunks, axis=0)
        on = o * lax.rsqrt(jnp.mean(o * o, axis=-1, keepdims=True) + EPS) * gnw_ref[...]
        yb_heads.append((on * rgate[:, vs]).astype(BF16))
    yb = jnp.concatenate(yb_heads, axis=1)

    bgt = bgt_ref[...]
    ga = _sigmoid(gates[:, :D_MODEL] + bgt[:, :D_MODEL]) * pa
    pb = _dot(yb, wb_ref[:, :D_MODEL])
    merged = ga + _sigmoid(gates[:, D_MODEL:] + bgt[:, D_MODEL:]) * pb
    h_ref[0] = x + _dot(merged.astype(BF16), wo_ref[:, :D_MODEL])


def _ffn_kernel(h_ref, n2w_ref, wg_ref, wu_ref, wd_ref, fnw_ref, out_ref, *, final_norm):
    tile = h_ref.shape[0]
    sub = tile // FFN_SUBTILES
    groups = [slice(i * sub, (i + 1) * sub) for i in range(FFN_SUBTILES)]
    hs, gs, us = [], [], []
    for rows in groups:
        h = h_ref[rows, :]
        xn = _rms_norm(h, n2w_ref[...]).astype(BF16)
        hs.append(h)
        gs.append(_dot(xn, wg_ref[...]))
        us.append(_dot(xn, wu_ref[...]))
    for rows, h, g, u in zip(groups, hs, gs, us):
        a = (g * jax.nn.sigmoid(g) * u).astype(BF16)
        h2 = h + _dot(a, wd_ref[...])
        if final_norm:
            h2 = _rms_norm(h2, fnw_ref[...])
        out_ref[rows, :] = h2


def _resident(shape):
    nd = len(shape)
    return pl.BlockSpec(shape, lambda *_: (0,) * nd, pipeline_mode=pl.Buffered(1))


def _cast_spec(shape, n_outer, n_inner):
    rows, cols = shape
    n_steps = n_outer * n_inner
    n_blocks = max(nb for nb in range(1, n_steps + 1)
                   if n_steps % nb == 0 and rows % (nb * BF16_SUBLANES) == 0)
    per_block = n_steps // n_blocks
    return pl.BlockSpec((rows // n_blocks, cols),
                        lambda b, j: ((b * n_inner + j) // per_block, 0))


def _mixer(x, params, cast_weights):
    batch, seq, _ = x.shape
    tile = MIXER_TILE
    assert seq % tile == 0 and tile % SGU_CHUNK == 0 and tile % GLA_CHUNK == 0
    n_tiles = seq // tile
    cast_specs = [_cast_spec(w.shape, batch, n_tiles) for w in cast_weights]
    in_specs = [pl.BlockSpec((1, tile, D_MODEL), lambda b, j: (b, j, 0))]
    in_specs += [_resident(p.shape) for p in params]
    in_specs += cast_specs
    return pl.pallas_call(
        _mixer_kernel,
        grid=(batch, n_tiles),
        in_specs=in_specs,
        out_specs=[pl.BlockSpec((1, tile, D_MODEL), lambda b, j: (b, j, 0))] + cast_specs,
        out_shape=[jax.ShapeDtypeStruct(x.shape, F32)]
        + [jax.ShapeDtypeStruct(w.shape, BF16) for w in cast_weights],
        scratch_shapes=[
            pltpu.VMEM((GLA_HEADS, GLA_HEAD_DK, GLA_HEAD_DV), F32),
        ],
        compiler_params=pltpu.CompilerParams(
            dimension_semantics=("arbitrary", "arbitrary"),
            vmem_limit_bytes=VMEM_LIMIT_BYTES),
        name="mixer",
    )(x, *params, *cast_weights)


def _ffn(h2d, params, final_norm):
    n_tok = h2d.shape[0]
    tile = FFN_TILE
    assert n_tok % tile == 0 and tile % (FFN_SUBTILES * SUBLANES) == 0
    in_specs = [pl.BlockSpec((tile, D_MODEL), lambda i: (i, 0))]
    in_specs += [_resident(p.shape) for p in params]
    return pl.pallas_call(
        functools.partial(_ffn_kernel, final_norm=final_norm),
        grid=(n_tok // tile,),
        in_specs=in_specs,
        out_specs=pl.BlockSpec((tile, D_MODEL), lambda i: (i, 0)),
        out_shape=jax.ShapeDtypeStruct(h2d.shape, F32),
        compiler_params=pltpu.CompilerParams(
            dimension_semantics=("arbitrary",),
            vmem_limit_bytes=VMEM_LIMIT_BYTES),
        name="ffn",
    )(h2d, *params)


def _wprep_kernel(uv_ref, qk_ref, vr0_ref, vr1_ref, g0_ref, g1_ref, g2_ref, wa_ref, wb_ref, wo_ref,
                  wuv_out, wqkl_out, wvr_out, wgt_out, wa_out, wb_out, wo_out):
    kb = uv_ref.shape[1]
    eye = (lax.broadcasted_iota(jnp.int32, (kb, kb), 0)
           == lax.broadcasted_iota(jnp.int32, (kb, kb), 1)).astype(BF16)

    def transposed(block):
        return lax.dot_general(eye, block.astype(BF16), (((1,), (1,)), ((), ())),
                               preferred_element_type=F32).astype(BF16)

    zero_tile = jnp.zeros((kb, LANES), BF16)

    def put(out_ref, body, tail):
        width = body.shape[1]
        out_ref[:, :width] = body
        out_ref[:, width:] = tail

    g0 = g0_ref[...]
    lane = lax.broadcasted_iota(jnp.int32, (kb, LANES), 1)
    lowrank = jnp.where(lane < GLA_LOWRANK, transposed(g0[:LANES]), 0)
    gate_rows = jnp.concatenate([g0[GLA_LOWRANK:], g1_ref[...], g2_ref[...]], axis=0)
    put(wuv_out, transposed(uv_ref[...]), zero_tile)
    put(wqkl_out, transposed(qk_ref[...]), lowrank)
    put(wvr_out, transposed(jnp.concatenate([vr0_ref[...], vr1_ref[...]], axis=0)), zero_tile)
    put(wgt_out, transposed(gate_rows), zero_tile)
    put(wa_out, wa_ref[...].astype(BF16), zero_tile)
    put(wb_out, wb_ref[...].astype(BF16), zero_tile)
    put(wo_out, wo_ref[...].astype(BF16), zero_tile)


def _wprep(w_in_t, w_a, w_b, w_o):
    kb = WPREP_ROWS
    assert D_MODEL % kb == 0
    half = D_MODEL
    assert O_QK == 2 * half and O_VR == 3 * half and O_LR == 5 * half
    tail = w_in_t.shape[0] - 7 * half
    assert tail == GLA_LOWRANK and (7 * half) % tail == 0

    def win(rows, block_index):
        return pl.BlockSpec((rows, kb), lambda i: (block_index, i))

    widths = [O_QK, O_VR - O_QK, O_LR - O_VR, 2 * D_MODEL, D_MODEL, D_MODEL, D_MODEL]
    row_block = lambda cols: pl.BlockSpec((kb, cols), lambda i: (i, 0))
    return pl.pallas_call(
        _wprep_kernel,
        grid=(D_MODEL // kb,),
        in_specs=[win(2 * half, 0), win(half, 2), win(half, 3), win(half, 4),
                  win(half, 5), win(half, 6), win(tail, 7 * half // tail)]
        + [row_block(D_MODEL)] * 3,
        out_specs=[row_block(n + LANES) for n in widths],
        out_shape=[jax.ShapeDtypeStruct((D_MODEL, n + LANES), BF16) for n in widths],
        compiler_params=pltpu.CompilerParams(
            dimension_semantics=("arbitrary",),
            vmem_limit_bytes=VMEM_LIMIT_BYTES),
        name="wprep",
    )(*([w_in_t] * 7), w_a, w_b, w_o)


def _row(v):
    return v.reshape(1, -1).astype(F32)


def kernel(x, norm1_w, w_in, b_in, sgu_ln_w, sgu_ln_b, sgu_w, sgu_b, w_gk2, b_gk, gla_norm_w,
           w_branch_a, w_branch_b, w_out, norm2_w, w_ffn_gate, w_ffn_up, w_ffn_down,
           final_norm_w):
    depth = w_in.shape[0]
    batch, seq, _ = x.shape
    h = x
    for l in range(depth):
        bl = b_in[l]
        lr_pad = LOWRANK_PAD - GLA_LOWRANK
        wuv, wqkl, wvr, wgt, wa, wb, wo = _wprep(w_in[l].T, w_branch_a[l], w_branch_b[l], w_out[l])
        mixer_params = [
            _row(norm1_w[l]),
            wuv, _row(bl[:O_QK]),
            wqkl, _row(jnp.concatenate([bl[O_QK:O_VR], bl[O_LR:O_GT],
                                        jnp.zeros((lr_pad,), bl.dtype)])),
            wvr, _row(bl[O_VR:O_LR]),
            wgt, _row(bl[O_GT:]),
            _row(sgu_ln_w[l]), _row(sgu_ln_b[l]),
            sgu_w[l].astype(F32),
            jnp.repeat(sgu_b[l].T.astype(F32), SGU_GROUP_DIM, axis=1),
            jnp.pad(w_gk2[l], ((0, lr_pad), (0, 0))).astype(BF16), _row(b_gk[l]),
            _row(gla_norm_w[l]),
            wa, wb, wo,
        ]
        h, wg, wu, wd = _mixer(h, mixer_params, [w_ffn_gate[l], w_ffn_up[l], w_ffn_down[l]])
        ffn_params = [_row(norm2_w[l]), wg, wu, wd, _row(final_norm_w)]
        h = _ffn(h.reshape(batch * seq, D_MODEL), ffn_params,
                 final_norm=(l == depth - 1)).reshape(batch, seq, D_MODEL)
    return h
```

```python
import functools
import math

import jax
import jax.numpy as jnp
from jax import lax
from jax.experimental import pallas as pl
from jax.experimental.pallas import tpu as pltpu

D_MODEL = 1024
EPS = 1e-6
SGU_CHUNK = 128
SGU_GROUPS = 8
SGU_GROUP_DIM = D_MODEL // SGU_GROUPS
GLA_HEADS = 4
GLA_DK = D_MODEL // 2
GLA_DV = D_MODEL
GLA_HEAD_DK = GLA_DK // GLA_HEADS
GLA_HEAD_DV = GLA_DV // GLA_HEADS
GLA_LOWRANK = 16
GLA_GATE_NORMALIZER = 16.0
GLA_CHUNK = 64
LANES = 128
SUBLANES = 8
BF16_SUBLANES = 16
LOWRANK_PAD = LANES
LOG2E = 1.4426950408889634
O_QK = 2 * D_MODEL
O_VR = O_QK + 2 * GLA_DK
O_LR = O_VR + 2 * GLA_DV
O_GT = O_LR + GLA_LOWRANK
WPREP_ROWS = 128

MIXER_TILE = 512
MIXER_SUBTILES = 2
FFN_TILE = 1024
FFN_SUBTILES = 4
VMEM_LIMIT_BYTES = 56 * 1024 * 1024

F32 = jnp.float32
BF16 = jnp.bfloat16


def _dot(a, b):
    return jnp.dot(a, b, preferred_element_type=F32)


def _gelu_tanh(x):
    c0 = math.sqrt(2.0 / math.pi)
    c1 = c0 * 0.044715
    t = jnp.tanh(x * (c0 + c1 * (x * x)))
    return x * (0.5 + 0.5 * t)


def _sigmoid(x):
    return 1.0 / (1.0 + jnp.exp2(x * (-LOG2E)))


def _silu(x):
    hx = 0.5 * x
    return hx + hx * jnp.tanh(hx)


def _rms_norm(x, w):
    return x * lax.rsqrt(jnp.mean(x * x, axis=-1, keepdims=True) + EPS) * w


def _mix_rows(x, states, n1w_ref, wuv_ref, buv_ref, wqkl_ref, bqkl_ref, wvr_ref, bvr_ref,
              wgt_ref, bgt_ref, lnw_ref, lnb_ref, sguw_ref, sgub_ref,
              wgk2_ref, bgk_ref, gnw_ref, wa_ref, wb_ref, wo_ref):
    tile = x.shape[0]
    n_sgu = tile // SGU_CHUNK
    n_gla = tile // GLA_CHUNK
    xn = _rms_norm(x, n1w_ref[...]).astype(BF16)

    qkl = _dot(xn, wqkl_ref[...])
    bqkl = bqkl_ref[...]
    glr = (qkl[:, 2 * GLA_DK:] + bqkl[:, 2 * GLA_DK:]).astype(BF16)
    z = _dot(glr, wgk2_ref[...]) + bgk_ref[...]
    uv = _dot(xn, wuv_ref[:, :2 * D_MODEL])
    vr = _dot(xn, wvr_ref[:, :2 * GLA_DV])
    gates = _dot(xn, wgt_ref[:, :2 * D_MODEL])

    soft = jnp.log2(1.0 + jnp.exp2(jnp.abs(z) * (-LOG2E)))
    g2 = (jnp.minimum(z, 0.0) * LOG2E - soft) * (1.0 / GLA_GATE_NORMALIZER)
    grp = 2 * GLA_CHUNK
    ri = lax.broadcasted_iota(jnp.int32, (grp, grp), 0)
    ci = lax.broadcasted_iota(jnp.int32, (grp, grp), 1)
    causal_blocks = ((ri // GLA_CHUNK == ci // GLA_CHUNK) & (ci <= ri)).astype(BF16)
    scan_lhs = jnp.concatenate([causal_blocks, causal_blocks], axis=1)
    g2_hi = g2.astype(BF16)
    g2_lo = (g2 - g2_hi.astype(F32)).astype(BF16)
    b2 = jnp.concatenate(
        [_dot(scan_lhs, jnp.concatenate([g2_hi[i * grp:(i + 1) * grp], g2_lo[i * grp:(i + 1) * grp]],
                                        axis=0))
         for i in range(tile // grp)], axis=0)
    last_rows = [b2[c * GLA_CHUNK + GLA_CHUNK - 1:(c + 1) * GLA_CHUNK, :] for c in range(n_gla)]
    b2_last = jnp.concatenate(
        [jnp.broadcast_to(r, (GLA_CHUNK, GLA_DK)) for r in last_rows], axis=0)
    q = qkl[:, :GLA_DK] + bqkl[:, :GLA_DK]
    k = qkl[:, GLA_DK:2 * GLA_DK] + bqkl[:, GLA_DK:2 * GLA_DK]
    q_dec = (q * (GLA_HEAD_DK ** -0.5) * jnp.exp2(b2)).astype(BF16)
    k_dec = (k * jnp.exp2(-b2)).astype(BF16)
    k_end = (k * jnp.exp2(b2_last - b2)).astype(BF16)
    pad_rows = (-n_gla) % SUBLANES
    last_mat = jnp.concatenate(last_rows + [jnp.zeros((pad_rows, GLA_DK), F32)] * (pad_rows > 0),
                               axis=0)
    decay_t = jnp.exp2(last_mat).T

    buv = buv_ref[...]
    gu = _gelu_tanh(uv[:, :D_MODEL] + buv[:, :D_MODEL])
    gv = _gelu_tanh(uv[:, D_MODEL:] + buv[:, D_MODEL:])
    mu = jnp.mean(gv, axis=-1, keepdims=True)
    cen = gv - mu
    var = jnp.mean(cen * cen, axis=-1, keepdims=True)
    vln = (cen * lax.rsqrt(var + EPS) * lnw_ref[...] + lnb_ref[...]).astype(BF16)
    row = lax.broadcasted_iota(jnp.int32, (SGU_CHUNK, SGU_CHUNK), 0)
    col = lax.broadcasted_iota(jnp.int32, (SGU_CHUNK, SGU_CHUNK), 1)
    causal = row >= col
    ya_groups = []
    for g in range(SGU_GROUPS):
        cs = slice(g * SGU_GROUP_DIM, (g + 1) * SGU_GROUP_DIM)
        wg = jnp.where(causal, sguw_ref[g], 0.0).astype(BF16)
        rhs = jnp.concatenate(
            [vln[c * SGU_CHUNK:(c + 1) * SGU_CHUNK, cs] for c in range(n_sgu)], axis=1)
        mixed = _dot(wg, rhs)
        mixed_rows = jnp.concatenate(
            [mixed[:, c * SGU_GROUP_DIM:(c + 1) * SGU_GROUP_DIM] + sgub_ref[:, cs]
             for c in range(n_sgu)], axis=0)
        ya_groups.append((gu[:, cs] * mixed_rows).astype(BF16))
    ya = jnp.concatenate(ya_groups, axis=1)
    pa = _dot(ya, wa_ref[:, :D_MODEL])

    bvr = bvr_ref[...]
    vb = (vr[:, :GLA_DV] + bvr[:, :GLA_DV]).astype(BF16)
    k_pad = jnp.concatenate([k_dec, jnp.zeros((GLA_CHUNK, GLA_DK), BF16)], axis=0)
    trow = lax.broadcasted_iota(jnp.int32, (GLA_CHUNK, 2 * GLA_CHUNK), 0)
    tcol = lax.broadcasted_iota(jnp.int32, (GLA_CHUNK, 2 * GLA_CHUNK), 1)
    tri = trow >= tcol
    zero_v = jnp.zeros((GLA_CHUNK, GLA_HEAD_DV), BF16)
    units = [(h, c) for h in range(GLA_HEADS) for c in range(n_gla)]
    lhs, kv = {}, {}
    for h, c in units:
        ks = slice(h * GLA_HEAD_DK, (h + 1) * GLA_HEAD_DK)
        vs = slice(h * GLA_HEAD_DV, (h + 1) * GLA_HEAD_DV)
        rs = slice(c * GLA_CHUNK, (c + 1) * GLA_CHUNK)
        qd = q_dec[rs, ks]
        scores = lax.dot_general(qd, k_pad[c * GLA_CHUNK:(c + 2) * GLA_CHUNK, ks],
                                 (((1,), (1,)), ((), ())), preferred_element_type=F32)
        scores = jnp.where(tri, scores, 0.0).astype(BF16)
        lhs[h, c] = jnp.concatenate([scores, qd], axis=1)
        kv[h, c] = lax.dot_general(k_end[rs, ks], vb[rs, vs], (((0,), (0,)), ((), ())),
                                   preferred_element_type=F32)
    r = vr[:, GLA_DV:] + bvr[:, GLA_DV:]
    rgate = _silu(r)
    yb_heads, new_states = [], []
    for h in range(GLA_HEADS):
        ks = slice(h * GLA_HEAD_DK, (h + 1) * GLA_HEAD_DK)
        vs = slice(h * GLA_HEAD_DV, (h + 1) * GLA_HEAD_DV)
        state = states[h]
        o_chunks = []
        for c in range(n_gla):
            rs = slice(c * GLA_CHUNK, (c + 1) * GLA_CHUNK)
            rhs = jnp.concatenate([vb[rs, vs], zero_v, state.astype(BF16)], axis=0)
            o_chunks.append(_dot(lhs[h, c], rhs))
            state = decay_t[ks, c:c + 1] * state + kv[h, c]
        new_states.append(state)
        o = jnp.concatenate(o_chunks, axis=0)
        on = o * lax.rsqrt(jnp.mean(o * o, axis=-1, keepdims=True) + EPS) * gnw_ref[...]
        yb_heads.append((on * rgate[:, vs]).astype(BF16))
    yb = jnp.concatenate(yb_heads, axis=1)

    bgt = bgt_ref[...]
    ga = _sigmoid(gates[:, :D_MODEL] + bgt[:, :D_MODEL]) * pa
    pb = _dot(yb, wb_ref[:, :D_MODEL])
    merged = ga + _sigmoid(gates[:, D_MODEL:] + bgt[:, D_MODEL:]) * pb
    return x + _dot(merged.astype(BF16), wo_ref[:, :D_MODEL]), new_states


def _mixer_kernel(x_ref, *refs):
    *param_refs, fg_ref, fu_ref, fd_ref, h_ref, fg_out_ref, fu_out_ref, fd_out_ref, state_ref = refs
    fg_out_ref[...] = fg_ref[...].astype(BF16)
    fu_out_ref[...] = fu_ref[...].astype(BF16)
    fd_out_ref[...] = fd_ref[...].astype(BF16)

    @pl.when(pl.program_id(1) == 0)
    def _():
        state_ref[...] = jnp.zeros_like(state_ref)

    sub = x_ref.shape[1] // MIXER_SUBTILES
    states = [state_ref[h] for h in range(GLA_HEADS)]
    for i in range(MIXER_SUBTILES):
        rows = slice(i * sub, (i + 1) * sub)
        h_rows, states = _mix_rows(x_ref[0, rows, :], states, *param_refs)
        h_ref[0, rows, :] = h_rows
    for h in range(GLA_HEADS):
        state_ref[h] = states[h]


def _ffn_kernel(h_ref, n2w_ref, wg_ref, wu_ref, wd_ref, fnw_ref, out_ref, *, final_norm):
    tile = h_ref.shape[0]
    sub = tile // FFN_SUBTILES
    groups = [slice(i * sub, (i + 1) * sub) for i in range(FFN_SUBTILES)]
    hs, gs, us = [], [], []
    for rows in groups:
        h = h_ref[rows, :]
        xn = _rms_norm(h, n2w_ref[...]).astype(BF16)
        hs.append(h)
        gs.append(_dot(xn, wg_ref[...]))
        us.append(_dot(xn, wu_ref[...]))
    for rows, h, g, u in zip(groups, hs, gs, us):
        a = (g * jax.nn.sigmoid(g) * u).astype(BF16)
        h2 = h + _dot(a, wd_ref[...])
        if final_norm:
            h2 = _rms_norm(h2, fnw_ref[...])
        out_ref[rows, :] = h2


def _resident(shape):
    nd = len(shape)
    return pl.BlockSpec(shape, lambda *_: (0,) * nd, pipeline_mode=pl.Buffered(1))


def _cast_spec(shape, n_outer, n_inner):
    rows, cols = shape
    n_steps = n_outer * n_inner
    n_blocks = max(nb for nb in range(1, n_steps + 1)
                   if n_steps % nb == 0 and rows % (nb * BF16_SUBLANES) == 0)
    per_block = n_steps // n_blocks
    return pl.BlockSpec((rows // n_blocks, cols),
                        lambda b, j: ((b * n_inner + j) // per_block, 0))


def _mixer(x, params, cast_weights):
    batch, seq, _ = x.shape
    tile = MIXER_TILE
    assert seq % tile == 0 and tile % (MIXER_SUBTILES * SGU_CHUNK) == 0
    assert SGU_CHUNK % GLA_CHUNK == 0
    n_tiles = seq // tile
    cast_specs = [_cast_spec(w.shape, batch, n_tiles) for w in cast_weights]
    in_specs = [pl.BlockSpec((1, tile, D_MODEL), lambda b, j: (b, j, 0))]
    in_specs += [_resident(p.shape) for p in params]
    in_specs += cast_specs
    return pl.pallas_call(
        _mixer_kernel,
        grid=(batch, n_tiles),
        in_specs=in_specs,
        out_specs=[pl.BlockSpec((1, tile, D_MODEL), lambda b, j: (b, j, 0))] + cast_specs,
        out_shape=[jax.ShapeDtypeStruct(x.shape, F32)]
        + [jax.ShapeDtypeStruct(w.shape, BF16) for w in cast_weights],
        scratch_shapes=[
            pltpu.VMEM((GLA_HEADS, GLA_HEAD_DK, GLA_HEAD_DV), F32),
        ],
        compiler_params=pltpu.CompilerParams(
            dimension_semantics=("arbitrary", "arbitrary"),
            vmem_limit_bytes=VMEM_LIMIT_BYTES),
        name="mixer",
    )(x, *params, *cast_weights)


def _ffn(h2d, params, final_norm):
    n_tok = h2d.shape[0]
    tile = FFN_TILE
    assert n_tok % tile == 0 and tile % (FFN_SUBTILES * SUBLANES) == 0
    in_specs = [pl.BlockSpec((tile, D_MODEL), lambda i: (i, 0))]
    in_specs += [_resident(p.shape) for p in params]
    return pl.pallas_call(
        functools.partial(_ffn_kernel, final_norm=final_norm),
        grid=(n_tok // tile,),
        in_specs=in_specs,
        out_specs=pl.BlockSpec((tile, D_MODEL), lambda i: (i, 0)),
        out_shape=jax.ShapeDtypeStruct(h2d.shape, F32),
        compiler_params=pltpu.CompilerParams(
            dimension_semantics=("arbitrary",),
            vmem_limit_bytes=VMEM_LIMIT_BYTES),
        name="ffn",
    )(h2d, *params)


def _wprep_kernel(uv_ref, qk_ref, vr0_ref, vr1_ref, g0_ref, g1_ref, g2_ref, wa_ref, wb_ref, wo_ref,
                  wuv_out, wqkl_out, wvr_out, wgt_out, wa_out, wb_out, wo_out):
    kb = uv_ref.shape[1]
    eye = (lax.broadcasted_iota(jnp.int32, (kb, kb), 0)
           == lax.broadcasted_iota(jnp.int32, (kb, kb), 1)).astype(BF16)

    def transposed(block):
        return lax.dot_general(eye, block.astype(BF16), (((1,), (1,)), ((), ())),
                               preferred_element_type=F32).astype(BF16)

    zero_tile = jnp.zeros((kb, LANES), BF16)

    def put(out_ref, body, tail):
        width = body.shape[1]
        out_ref[:, :width] = body
        out_ref[:, width:] = tail

    g0 = g0_ref[...]
    lane = lax.broadcasted_iota(jnp.int32, (kb, LANES), 1)
    lowrank = jnp.where(lane < GLA_LOWRANK, transposed(g0[:LANES]), 0)
    gate_rows = jnp.concatenate([g0[GLA_LOWRANK:], g1_ref[...], g2_ref[...]], axis=0)
    put(wuv_out, transposed(uv_ref[...]), zero_tile)
    put(wqkl_out, transposed(qk_ref[...]), lowrank)
    put(wvr_out, transposed(jnp.concatenate([vr0_ref[...], vr1_ref[...]], axis=0)), zero_tile)
    put(wgt_out, transposed(gate_rows), zero_tile)
    put(wa_out, wa_ref[...].astype(BF16), zero_tile)
    put(wb_out, wb_ref[...].astype(BF16), zero_tile)
    put(wo_out, wo_ref[...].astype(BF16), zero_tile)


def _wprep(w_in_t, w_a, w_b, w_o):
    kb = WPREP_ROWS
    assert D_MODEL % kb == 0
    half = D_MODEL
    assert O_QK == 2 * half and O_VR == 3 * half and O_LR == 5 * half
    tail = w_in_t.shape[0] - 7 * half
    assert tail == GLA_LOWRANK and (7 * half) % tail == 0

    def win(rows, block_index):
        return pl.BlockSpec((rows, kb), lambda i: (block_index, i))

    widths = [O_QK, O_VR - O_QK, O_LR - O_VR, 2 * D_MODEL, D_MODEL, D_MODEL, D_MODEL]
    row_block = lambda cols: pl.BlockSpec((kb, cols), lambda i: (i, 0))
    return pl.pallas_call(
        _wprep_kernel,
        grid=(D_MODEL // kb,),
        in_specs=[win(2 * half, 0), win(half, 2), win(half, 3), win(half, 4),
                  win(half, 5), win(half, 6), win(tail, 7 * half // tail)]
        + [row_block(D_MODEL)] * 3,
        out_specs=[row_block(n + LANES) for n in widths],
        out_shape=[jax.ShapeDtypeStruct((D_MODEL, n + LANES), BF16) for n in widths],
        compiler_params=pltpu.CompilerParams(
            dimension_semantics=("arbitrary",),
            vmem_limit_bytes=VMEM_LIMIT_BYTES),
        name="wprep",
    )(*([w_in_t] * 7), w_a, w_b, w_o)


def _row(v):
    return v.reshape(1, -1).astype(F32)


def kernel(x, norm1_w, w_in, b_in, sgu_ln_w, sgu_ln_b, sgu_w, sgu_b, w_gk2, b_gk, gla_norm_w,
           w_branch_a, w_branch_b, w_out, norm2_w, w_ffn_gate, w_ffn_up, w_ffn_down,
           final_norm_w):
    depth = w_in.shape[0]
    batch, seq, _ = x.shape
    h = x
    for l in range(depth):
        bl = b_in[l]
        lr_pad = LOWRANK_PAD - GLA_LOWRANK
        wuv, wqkl, wvr, wgt, wa, wb, wo = _wprep(w_in[l].T, w_branch_a[l], w_branch_b[l], w_out[l])
        mixer_params = [
            _row(norm1_w[l]),
            wuv, _row(bl[:O_QK]),
            wqkl, _row(jnp.concatenate([bl[O_QK:O_VR], bl[O_LR:O_GT],
                                        jnp.zeros((lr_pad,), bl.dtype)])),
            wvr, _row(bl[O_VR:O_LR]),
            wgt, _row(bl[O_GT:]),
            _row(sgu_ln_w[l]), _row(sgu_ln_b[l]),
            sgu_w[l].astype(F32),
            jnp.repeat(sgu_b[l].T.astype(F32), SGU_GROUP_DIM, axis=1),
            jnp.pad(w_gk2[l], ((0, lr_pad), (0, 0))).astype(BF16), _row(b_gk[l]),
            _row(gla_norm_w[l]),
            wa, wb, wo,
        ]
        h, wg, wu, wd = _mixer(h, mixer_params, [w_ffn_gate[l], w_ffn_up[l], w_ffn_down[l]])
        ffn_params = [_row(norm2_w[l]), wg, wu, wd, _row(final_norm_w)]
        h = _ffn(h.reshape(batch * seq, D_MODEL), ffn_params,
                 final_norm=(l == depth - 1)).reshape(batch, seq, D_MODEL)
    return h
```

```python
import functools
import math

import jax
import jax.numpy as jnp
from jax import lax
from jax.experimental import pallas as pl
from jax.experimental.pallas import tpu as pltpu

D_MODEL = 1024
EPS = 1e-6
SGU_CHUNK = 128
SGU_GROUPS = 8
SGU_GROUP_DIM = D_MODEL // SGU_GROUPS
GLA_HEADS = 4
GLA_DK = D_MODEL // 2
GLA_DV = D_MODEL
GLA_HEAD_DK = GLA_DK // GLA_HEADS
GLA_HEAD_DV = GLA_DV // GLA_HEADS
GLA_LOWRANK = 16
GLA_GATE_NORMALIZER = 16.0
GLA_CHUNK = 64
LANES = 128
SUBLANES = 8
BF16_SUBLANES = 16
LOWRANK_PAD = LANES
LOG2E = 1.4426950408889634
O_QK = 2 * D_MODEL
O_VR = O_QK + 2 * GLA_DK
O_LR = O_VR + 2 * GLA_DV
O_GT = O_LR + GLA_LOWRANK
WPREP_ROWS = 128

MIXER_TILE = 512
MIXER_SUBTILES = 1
FFN_TILE = 1024
FFN_SUBTILES = 4
VMEM_LIMIT_BYTES = 56 * 1024 * 1024

F32 = jnp.float32
BF16 = jnp.bfloat16


def _dot(a, b):
    return jnp.dot(a, b, preferred_element_type=F32)


def _gelu_tanh(x):
    c0 = math.sqrt(2.0 / math.pi)
    c1 = c0 * 0.044715
    t = jnp.tanh(x * (c0 + c1 * (x * x)))
    return x * (0.5 + 0.5 * t)


def _sigmoid(x):
    return 1.0 / (1.0 + jnp.exp2(x * (-LOG2E)))


def _silu(x):
    hx = 0.5 * x
    return hx + hx * jnp.tanh(hx)


def _rms_norm(x, w):
    return x * lax.rsqrt(jnp.mean(x * x, axis=-1, keepdims=True) + EPS) * w


def _mix_rows(x, states, n1w_ref, wuv_ref, buv_ref, wqkl_ref, bqkl_ref, wvr_ref, bvr_ref,
              wgt_ref, bgt_ref, lnw_ref, lnb_ref, sguw_ref, sgub_ref,
              wgk2_ref, bgk_ref, gnw_ref, wa_ref, wb_ref, wo_ref):
    tile = x.shape[0]
    n_sgu = tile // SGU_CHUNK
    n_gla = tile // GLA_CHUNK
    xn = _rms_norm(x, n1w_ref[...]).astype(BF16)

    qkl = _dot(xn, wqkl_ref[...])
    bqkl = bqkl_ref[...]
    glr = (qkl[:, 2 * GLA_DK:] + bqkl[:, 2 * GLA_DK:]).astype(BF16)
    z = _dot(glr, wgk2_ref[...]) + bgk_ref[...]
    uv = _dot(xn, wuv_ref[:, :2 * D_MODEL])
    vr = _dot(xn, wvr_ref[:, :2 * GLA_DV])
    gates = _dot(xn, wgt_ref[:, :2 * D_MODEL])

    soft = jnp.log2(1.0 + jnp.exp2(jnp.abs(z) * (-LOG2E)))
    g2 = (jnp.minimum(z, 0.0) * LOG2E - soft) * (1.0 / GLA_GATE_NORMALIZER)
    grp = 2 * GLA_CHUNK
    ri = lax.broadcasted_iota(jnp.int32, (grp, grp), 0)
    ci = lax.broadcasted_iota(jnp.int32, (grp, grp), 1)
    causal_blocks = ((ri // GLA_CHUNK == ci // GLA_CHUNK) & (ci <= ri)).astype(BF16)
    scan_lhs = jnp.concatenate([causal_blocks, causal_blocks], axis=1)
    g2_hi = g2.astype(BF16)
    g2_lo = (g2 - g2_hi.astype(F32)).astype(BF16)
    b2 = jnp.concatenate(
        [_dot(scan_lhs, jnp.concatenate([g2_hi[i * grp:(i + 1) * grp], g2_lo[i * grp:(i + 1) * grp]],
                                        axis=0))
         for i in range(tile // grp)], axis=0)
    last_rows = [b2[c * GLA_CHUNK + GLA_CHUNK - 1:(c + 1) * GLA_CHUNK, :] for c in range(n_gla)]
    b2_last = jnp.concatenate(
        [jnp.broadcast_to(r, (GLA_CHUNK, GLA_DK)) for r in last_rows], axis=0)
    q = qkl[:, :GLA_DK] + bqkl[:, :GLA_DK]
    k = qkl[:, GLA_DK:2 * GLA_DK] + bqkl[:, GLA_DK:2 * GLA_DK]
    q_dec = (q * (GLA_HEAD_DK ** -0.5) * jnp.exp2(b2)).astype(BF16)
    k_dec = (k * jnp.exp2(-b2)).astype(BF16)
    k_end = (k * jnp.exp2(b2_last - b2)).astype(BF16)
    pad_rows = (-n_gla) % SUBLANES
    last_mat = jnp.concatenate(last_rows + [jnp.zeros((pad_rows, GLA_DK), F32)] * (pad_rows > 0),
                               axis=0)
    decay_t = jnp.exp2(last_mat).T

    buv = buv_ref[...]
    gu = _gelu_tanh(uv[:, :D_MODEL] + buv[:, :D_MODEL])
    gv = _gelu_tanh(uv[:, D_MODEL:] + buv[:, D_MODEL:])
    mu = jnp.mean(gv, axis=-1, keepdims=True)
    cen = gv - mu
    var = jnp.mean(cen * cen, axis=-1, keepdims=True)
    vln = (cen * lax.rsqrt(var + EPS) * lnw_ref[...] + lnb_ref[...]).astype(BF16)
    row = lax.broadcasted_iota(jnp.int32, (SGU_CHUNK, SGU_CHUNK), 0)
    col = lax.broadcasted_iota(jnp.int32, (SGU_CHUNK, SGU_CHUNK), 1)
    causal = row >= col
    ya_groups = []
    for g in range(SGU_GROUPS):
        cs = slice(g * SGU_GROUP_DIM, (g + 1) * SGU_GROUP_DIM)
        wg = jnp.where(causal, sguw_ref[g], 0.0).astype(BF16)
        rhs = jnp.concatenate(
            [vln[c * SGU_CHUNK:(c + 1) * SGU_CHUNK, cs] for c in range(n_sgu)], axis=1)
        mixed = _dot(wg, rhs)
        mixed_rows = jnp.concatenate(
            [mixed[:, c * SGU_GROUP_DIM:(c + 1) * SGU_GROUP_DIM] + sgub_ref[:, cs]
             for c in range(n_sgu)], axis=0)
        ya_groups.append((gu[:, cs] * mixed_rows).astype(BF16))
    ya = jnp.concatenate(ya_groups, axis=1)
    pa = _dot(ya, wa_ref[:, :D_MODEL])

    bvr = bvr_ref[...]
    vb = (vr[:, :GLA_DV] + bvr[:, :GLA_DV]).astype(BF16)
    k_pad = jnp.concatenate([k_dec, jnp.zeros((GLA_CHUNK, GLA_DK), BF16)], axis=0)
    trow = lax.broadcasted_iota(jnp.int32, (GLA_CHUNK, 2 * GLA_CHUNK), 0)
    tcol = lax.broadcasted_iota(jnp.int32, (GLA_CHUNK, 2 * GLA_CHUNK), 1)
    tri = trow >= tcol
    zero_v = jnp.zeros((GLA_CHUNK, GLA_HEAD_DV), BF16)
    units = [(h, c) for h in range(GLA_HEADS) for c in range(n_gla)]
    lhs, kv = {}, {}
    for h, c in units:
        ks = slice(h * GLA_HEAD_DK, (h + 1) * GLA_HEAD_DK)
        vs = slice(h * GLA_HEAD_DV, (h + 1) * GLA_HEAD_DV)
        rs = slice(c * GLA_CHUNK, (c + 1) * GLA_CHUNK)
        qd = q_dec[rs, ks]
        scores = lax.dot_general(qd, k_pad[c * GLA_CHUNK:(c + 2) * GLA_CHUNK, ks],
                                 (((1,), (1,)), ((), ())), preferred_element_type=F32)
        scores = jnp.where(tri, scores, 0.0).astype(BF16)
        lhs[h, c] = jnp.concatenate([scores, qd], axis=1)
        kv[h, c] = lax.dot_general(k_end[rs, ks], vb[rs, vs], (((0,), (0,)), ((), ())),
                                   preferred_element_type=F32)
    r = vr[:, GLA_DV:] + bvr[:, GLA_DV:]
    rgate = _silu(r)
    yb_heads, new_states = [], []
    for h in range(GLA_HEADS):
        ks = slice(h * GLA_HEAD_DK, (h + 1) * GLA_HEAD_DK)
        vs = slice(h * GLA_HEAD_DV, (h + 1) * GLA_HEAD_DV)
        state = states[h]
        o_chunks = []
        for c in range(n_gla):
            rs = slice(c * GLA_CHUNK, (c + 1) * GLA_CHUNK)
            rhs = jnp.concatenate([vb[rs, vs], zero_v, state.astype(BF16)], axis=0)
            o_chunks.append(_dot(lhs[h, c], rhs))
            state = decay_t[ks, c:c + 1] * state + kv[h, c]
        new_states.append(state)
        o = jnp.concatenate(o_chunks, axis=0)
        on = o * lax.rsqrt(jnp.mean(o * o, axis=-1, keepdims=True) + EPS) * gnw_ref[...]
        yb_heads.append((on * rgate[:, vs]).astype(BF16))
    yb = jnp.concatenate(yb_heads, axis=1)

    bgt = bgt_ref[...]
    ga = _sigmoid(gates[:, :D_MODEL] + bgt[:, :D_MODEL]) * pa
    pb = _dot(yb, wb_ref[:, :D_MODEL])
    merged = ga + _sigmoid(gates[:, D_MODEL:] + bgt[:, D_MODEL:]) * pb
    return x + _dot(merged.astype(BF16), wo_ref[:, :D_MODEL]), new_states


def _mixer_kernel(x_ref, *refs):
    *param_refs, fg_ref, fu_ref, fd_ref, h_ref, fg_out_ref, fu_out_ref, fd_out_ref, state_ref = refs
    fg_out_ref[...] = fg_ref[...].astype(BF16)
    fu_out_ref[...] = fu_ref[...].astype(BF16)
    fd_out_ref[...] = fd_ref[...].astype(BF16)

    @pl.when(pl.program_id(1) == 0)
    def _():
        state_ref[...] = jnp.zeros_like(state_ref)

    sub = x_ref.shape[1] // MIXER_SUBTILES
    states = [state_ref[h] for h in range(GLA_HEADS)]
    for i in range(MIXER_SUBTILES):
        rows = slice(i * sub, (i + 1) * sub)
        h_rows, states = _mix_rows(x_ref[0, rows, :], states, *param_refs)
        h_ref[0, rows, :] = h_rows
    for h in range(GLA_HEADS):
        state_ref[h] = states[h]


def _ffn_kernel(h_ref, n2w_ref, wg_ref, wu_ref, wd_ref, fnw_ref, out_ref, *, final_norm):
    tile = h_ref.shape[0]
    sub = tile // FFN_SUBTILES
    groups = [slice(i * sub, (i + 1) * sub) for i in range(FFN_SUBTILES)]
    hs, gs, us = [], [], []
    for rows in groups:
        h = h_ref[rows, :]
        xn = _rms_norm(h, n2w_ref[...]).astype(BF16)
        hs.append(h)
        gs.append(_dot(xn, wg_ref[...]))
        us.append(_dot(xn, wu_ref[...]))
    for rows, h, g, u in zip(groups, hs, gs, us):
        a = (g * jax.nn.sigmoid(g) * u).astype(BF16)
        h2 = h + _dot(a, wd_ref[...])
        if final_norm:
            h2 = _rms_norm(h2, fnw_ref[...])
        out_ref[rows, :] = h2


def _resident(shape):
    nd = len(shape)
    return pl.BlockSpec(shape, lambda *_: (0,) * nd, pipeline_mode=pl.Buffered(1))


def _cast_spec(shape, n_outer, n_inner):
    rows, cols = shape
    n_steps = n_outer * n_inner
    n_blocks = max(nb for nb in range(1, n_steps + 1)
                   if n_steps % nb == 0 and rows % (nb * BF16_SUBLANES) == 0)
    per_block = n_steps // n_blocks
    return pl.BlockSpec((rows // n_blocks, cols),
                        lambda b, j: ((b * n_inner + j) // per_block, 0))


def _mixer(x, params, cast_weights):
    batch, seq, _ = x.shape
    tile = MIXER_TILE
    assert seq % tile == 0 and tile % (MIXER_SUBTILES * SGU_CHUNK) == 0
    assert SGU_CHUNK % GLA_CHUNK == 0
    n_tiles = seq // tile
    cast_specs = [_cast_spec(w.shape, batch, n_tiles) for w in cast_weights]
    in_specs = [pl.BlockSpec((1, tile, D_MODEL), lambda b, j: (b, j, 0))]
    in_specs += [_resident(p.shape) for p in params]
    in_specs += cast_specs
    return pl.pallas_call(
        _mixer_kernel,
        grid=(batch, n_tiles),
        in_specs=in_specs,
        out_specs=[pl.BlockSpec((1, tile, D_MODEL), lambda b, j: (b, j, 0))] + cast_specs,
        out_shape=[jax.ShapeDtypeStruct(x.shape, F32)]
        + [jax.ShapeDtypeStruct(w.shape, BF16) for w in cast_weights],
        scratch_shapes=[
            pltpu.VMEM((GLA_HEADS, GLA_HEAD_DK, GLA_HEAD_DV), F32),
        ],
        compiler_params=pltpu.CompilerParams(
            dimension_semantics=("arbitrary", "arbitrary"),
            vmem_limit_bytes=VMEM_LIMIT_BYTES),
        name="mixer",
    )(x, *params, *cast_weights)


def _ffn(h2d, params, final_norm):
    n_tok = h2d.shape[0]
    tile = FFN_TILE
    assert n_tok % tile == 0 and tile % (FFN_SUBTILES * SUBLANES) == 0
    in_specs = [pl.BlockSpec((tile, D_MODEL), lambda i: (i, 0))]
    in_specs += [_resident(p.shape) for p in params]
    return pl.pallas_call(
        functools.partial(_ffn_kernel, final_norm=final_norm),
        grid=(n_tok // tile,),
        in_specs=in_specs,
        out_specs=pl.BlockSpec((tile, D_MODEL), lambda i: (i, 0)),
        out_shape=jax.ShapeDtypeStruct(h2d.shape, F32),
        compiler_params=pltpu.CompilerParams(
            dimension_semantics=("arbitrary",),
            vmem_limit_bytes=VMEM_LIMIT_BYTES),
        name="ffn",
    )(h2d, *params)


def _wprep_kernel(uv_ref, qk_ref, vr0_ref, vr1_ref, g0_ref, g1_ref, g2_ref, wa_ref, wb_ref, wo_ref,
                  wuv_out, wqkl_out, wvr_out, wgt_out, wa_out, wb_out, wo_out):
    kb = uv_ref.shape[1]
    eye = (lax.broadcasted_iota(jnp.int32, (kb, kb), 0)
           == lax.broadcasted_iota(jnp.int32, (kb, kb), 1)).astype(BF16)

    def transposed(block):
        return lax.dot_general(eye, block.astype(BF16), (((1,), (1,)), ((), ())),
                               preferred_element_type=F32).astype(BF16)

    zero_tile = jnp.zeros((kb, LANES), BF16)

    def put(out_ref, body, tail):
        width = body.shape[1]
        out_ref[:, :width] = body
        out_ref[:, width:] = tail

    g0 = g0_ref[...]
    lane = lax.broadcasted_iota(jnp.int32, (kb, LANES), 1)
    lowrank = jnp.where(lane < GLA_LOWRANK, transposed(g0[:LANES]), 0)
    gate_rows = jnp.concatenate([g0[GLA_LOWRANK:], g1_ref[...], g2_ref[...]], axis=0)
    put(wuv_out, transposed(uv_ref[...]), zero_tile)
    put(wqkl_out, transposed(qk_ref[...]), lowrank)
    put(wvr_out, transposed(jnp.concatenate([vr0_ref[...], vr1_ref[...]], axis=0)), zero_tile)
    put(wgt_out, transposed(gate_rows), zero_tile)
    put(wa_out, wa_ref[...].astype(BF16), zero_tile)
    put(wb_out, wb_ref[...].astype(BF16), zero_tile)
    put(wo_out, wo_ref[...].astype(BF16), zero_tile)


def _wprep(w_in_t, w_a, w_b, w_o):
    kb = WPREP_ROWS
    assert D_MODEL % kb == 0
    half = D_MODEL
    assert O_QK == 2 * half and O_VR == 3 * half and O_LR == 5 * half
    tail = w_in_t.shape[0] - 7 * half
    assert tail == GLA_LOWRANK and (7 * half) % tail == 0

    def win(rows, block_index):
        return pl.BlockSpec((rows, kb), lambda i: (block_index, i))

    widths = [O_QK, O_VR - O_QK, O_LR - O_VR, 2 * D_MODEL, D_MODEL, D_MODEL, D_MODEL]
    row_block = lambda cols: pl.BlockSpec((kb, cols), lambda i: (i, 0))
    return pl.pallas_call(
        _wprep_kernel,
        grid=(D_MODEL // kb,),
        in_specs=[win(2 * half, 0), win(half, 2), win(half, 3), win(half, 4),
                  win(half, 5), win(half, 6), win(tail, 7 * half // tail)]
        + [row_block(D_MODEL)] * 3,
        out_specs=[row_block(n + LANES) for n in widths],
        out_shape=[jax.ShapeDtypeStruct((D_MODEL, n + LANES), BF16) for n in widths],
        compiler_params=pltpu.CompilerParams(
            dimension_semantics=("arbitrary",),
            vmem_limit_bytes=VMEM_LIMIT_BYTES),
        name="wprep",
    )(*([w_in_t] * 7), w_a, w_b, w_o)


def _row(v):
    return v.reshape(1, -1).astype(F32)


def kernel(x, norm1_w, w_in, b_in, sgu_ln_w, sgu_ln_b, sgu_w, sgu_b, w_gk2, b_gk, gla_norm_w,
           w_branch_a, w_branch_b, w_out, norm2_w, w_ffn_gate, w_ffn_up, w_ffn_down,
           final_norm_w):
    depth = w_in.shape[0]
    batch, seq, _ = x.shape
    h = x
    for l in range(depth):
        bl = b_in[l]
        lr_pad = LOWRANK_PAD - GLA_LOWRANK
        wuv, wqkl, wvr, wgt, wa, wb, wo = _wprep(w_in[l].T, w_branch_a[l], w_branch_b[l], w_out[l])
        mixer_params = [
            _row(norm1_w[l]),
            wuv, _row(bl[:O_QK]),
            wqkl, _row(jnp.concatenate([bl[O_QK:O_VR], bl[O_LR:O_GT],
                                        jnp.zeros((lr_pad,), bl.dtype)])),
            wvr, _row(bl[O_VR:O_LR]),
            wgt, _row(bl[O_GT:]),
            _row(sgu_ln_w[l]), _row(sgu_ln_b[l]),
            sgu_w[l].astype(F32),
            jnp.repeat(sgu_b[l].T.astype(F32), SGU_GROUP_DIM, axis=1),
            jnp.pad(w_gk2[l], ((0, lr_pad), (0, 0))).astype(BF16), _row(b_gk[l]),
            _row(gla_norm_w[l]),
            wa, wb, wo,
        ]
        h, wg, wu, wd = _mixer(h, mixer_params, [w_ffn_gate[l], w_ffn_up[l], w_ffn_down[l]])
        ffn_params = [_row(norm2_w[l]), wg, wu, wd, _row(final_norm_w)]
        h = _ffn(h.reshape(batch * seq, D_MODEL), ffn_params,
                 final_norm=(l == depth - 1)).reshape(batch, seq, D_MODEL)
    return h
```

```python
import functools
import math

import jax
import jax.numpy as jnp
from jax import lax
from jax.experimental import pallas as pl
from jax.experimental.pallas import tpu as pltpu

D_MODEL = 1024
EPS = 1e-6
SGU_CHUNK = 128
SGU_GROUPS = 8
SGU_GROUP_DIM = D_MODEL // SGU_GROUPS
GLA_HEADS = 4
GLA_DK = D_MODEL // 2
GLA_DV = D_MODEL
GLA_HEAD_DK = GLA_DK // GLA_HEADS
GLA_HEAD_DV = GLA_DV // GLA_HEADS
GLA_LOWRANK = 16
GLA_GATE_NORMALIZER = 16.0
GLA_CHUNK = 64
LANES = 128
SUBLANES = 8
BF16_SUBLANES = 16
LOWRANK_PAD = LANES
LOG2E = 1.4426950408889634
O_QK = 2 * D_MODEL
O_VR = O_QK + 2 * GLA_DK
O_LR = O_VR + 2 * GLA_DV
O_GT = O_LR + GLA_LOWRANK
WPREP_ROWS = 128

MIXER_TILE = 512
MIXER_SUBTILES = 1
FFN_TILE = 1024
FFN_SUBTILES = 2
VMEM_LIMIT_BYTES = 56 * 1024 * 1024

F32 = jnp.float32
BF16 = jnp.bfloat16


def _dot(a, b):
    return jnp.dot(a, b, preferred_element_type=F32)


def _gelu_tanh(x):
    c0 = math.sqrt(2.0 / math.pi)
    c1 = c0 * 0.044715
    t = jnp.tanh(x * (c0 + c1 * (x * x)))
    return x * (0.5 + 0.5 * t)


def _sigmoid(x):
    return 1.0 / (1.0 + jnp.exp2(x * (-LOG2E)))


def _silu(x):
    hx = 0.5 * x
    return hx + hx * jnp.tanh(hx)


def _rms_norm(x, w):
    return x * lax.rsqrt(jnp.mean(x * x, axis=-1, keepdims=True) + EPS) * w


def _mix_rows(x, states, n1w_ref, wuv_ref, buv_ref, wqkl_ref, bqkl_ref, wvr_ref, bvr_ref,
              wgt_ref, bgt_ref, lnw_ref, lnb_ref, sguw_ref, sgub_ref,
              wgk2_ref, bgk_ref, gnw_ref, wa_ref, wb_ref, wo_ref):
    tile = x.shape[0]
    n_sgu = tile // SGU_CHUNK
    n_gla = tile // GLA_CHUNK
    xn = _rms_norm(x, n1w_ref[...]).astype(BF16)

    qkl = _dot(xn, wqkl_ref[...])
    bqkl = bqkl_ref[...]
    glr = (qkl[:, 2 * GLA_DK:] + bqkl[:, 2 * GLA_DK:]).astype(BF16)
    z = _dot(glr, wgk2_ref[...]) + bgk_ref[...]
    uv = _dot(xn, wuv_ref[:, :2 * D_MODEL])
    vr = _dot(xn, wvr_ref[:, :2 * GLA_DV])
    gates = _dot(xn, wgt_ref[:, :2 * D_MODEL])

    soft = jnp.log2(1.0 + jnp.exp2(jnp.abs(z) * (-LOG2E)))
    g2 = (jnp.minimum(z, 0.0) * LOG2E - soft) * (1.0 / GLA_GATE_NORMALIZER)
    grp = 2 * GLA_CHUNK
    ri = lax.broadcasted_iota(jnp.int32, (grp, grp), 0)
    ci = lax.broadcasted_iota(jnp.int32, (grp, grp), 1)
    causal_blocks = ((ri // GLA_CHUNK == ci // GLA_CHUNK) & (ci <= ri)).astype(BF16)
    scan_lhs = jnp.concatenate([causal_blocks, causal_blocks], axis=1)
    g2_hi = g2.astype(BF16)
    g2_lo = (g2 - g2_hi.astype(F32)).astype(BF16)
    b2 = jnp.concatenate(
        [_dot(scan_lhs, jnp.concatenate([g2_hi[i * grp:(i + 1) * grp], g2_lo[i * grp:(i + 1) * grp]],
                                        axis=0))
         for i in range(tile // grp)], axis=0)
    last_rows = [b2[c * GLA_CHUNK + GLA_CHUNK - 1:(c + 1) * GLA_CHUNK, :] for c in range(n_gla)]
    b2_last = jnp.concatenate(
        [jnp.broadcast_to(r, (GLA_CHUNK, GLA_DK)) for r in last_rows], axis=0)
    q = qkl[:, :GLA_DK] + bqkl[:, :GLA_DK]
    k = qkl[:, GLA_DK:2 * GLA_DK] + bqkl[:, GLA_DK:2 * GLA_DK]
    q_dec = (q * (GLA_HEAD_DK ** -0.5) * jnp.exp2(b2)).astype(BF16)
    k_dec = (k * jnp.exp2(-b2)).astype(BF16)
    k_end = (k * jnp.exp2(b2_last - b2)).astype(BF16)
    pad_rows = (-n_gla) % SUBLANES
    last_mat = jnp.concatenate(last_rows + [jnp.zeros((pad_rows, GLA_DK), F32)] * (pad_rows > 0),
                               axis=0)
    decay_t = jnp.exp2(last_mat).T

    buv = buv_ref[...]
    gu = _gelu_tanh(uv[:, :D_MODEL] + buv[:, :D_MODEL])
    gv = _gelu_tanh(uv[:, D_MODEL:] + buv[:, D_MODEL:])
    mu = jnp.mean(gv, axis=-1, keepdims=True)
    cen = gv - mu
    var = jnp.mean(cen * cen, axis=-1, keepdims=True)
    vln = (cen * lax.rsqrt(var + EPS) * lnw_ref[...] + lnb_ref[...]).astype(BF16)
    row = lax.broadcasted_iota(jnp.int32, (SGU_CHUNK, SGU_CHUNK), 0)
    col = lax.broadcasted_iota(jnp.int32, (SGU_CHUNK, SGU_CHUNK), 1)
    causal = row >= col
    ya_groups = []
    for g in range(SGU_GROUPS):
        cs = slice(g * SGU_GROUP_DIM, (g + 1) * SGU_GROUP_DIM)
        wg = jnp.where(causal, sguw_ref[g], 0.0).astype(BF16)
        rhs = jnp.concatenate(
            [vln[c * SGU_CHUNK:(c + 1) * SGU_CHUNK, cs] for c in range(n_sgu)], axis=1)
        mixed = _dot(wg, rhs)
        mixed_rows = jnp.concatenate(
            [mixed[:, c * SGU_GROUP_DIM:(c + 1) * SGU_GROUP_DIM] + sgub_ref[:, cs]
             for c in range(n_sgu)], axis=0)
        ya_groups.append((gu[:, cs] * mixed_rows).astype(BF16))
    ya = jnp.concatenate(ya_groups, axis=1)
    pa = _dot(ya, wa_ref[:, :D_MODEL])

    bvr = bvr_ref[...]
    vb = (vr[:, :GLA_DV] + bvr[:, :GLA_DV]).astype(BF16)
    k_pad = jnp.concatenate([k_dec, jnp.zeros((GLA_CHUNK, GLA_DK), BF16)], axis=0)
    trow = lax.broadcasted_iota(jnp.int32, (GLA_CHUNK, 2 * GLA_CHUNK), 0)
    tcol = lax.broadcasted_iota(jnp.int32, (GLA_CHUNK, 2 * GLA_CHUNK), 1)
    tri = trow >= tcol
    zero_v = jnp.zeros((GLA_CHUNK, GLA_HEAD_DV), BF16)
    units = [(h, c) for h in range(GLA_HEADS) for c in range(n_gla)]
    lhs, kv = {}, {}
    for h, c in units:
        ks = slice(h * GLA_HEAD_DK, (h + 1) * GLA_HEAD_DK)
        vs = slice(h * GLA_HEAD_DV, (h + 1) * GLA_HEAD_DV)
        rs = slice(c * GLA_CHUNK, (c + 1) * GLA_CHUNK)
        qd = q_dec[rs, ks]
        scores = lax.dot_general(qd, k_pad[c * GLA_CHUNK:(c + 2) * GLA_CHUNK, ks],
                                 (((1,), (1,)), ((), ())), preferred_element_type=F32)
        scores = jnp.where(tri, scores, 0.0).astype(BF16)
        lhs[h, c] = jnp.concatenate([scores, qd], axis=1)
        kv[h, c] = lax.dot_general(k_end[rs, ks], vb[rs, vs], (((0,), (0,)), ((), ())),
                                   preferred_element_type=F32)
    r = vr[:, GLA_DV:] + bvr[:, GLA_DV:]
    rgate = _silu(r)
    yb_heads, new_states = [], []
    for h in range(GLA_HEADS):
        ks = slice(h * GLA_HEAD_DK, (h + 1) * GLA_HEAD_DK)
        vs = slice(h * GLA_HEAD_DV, (h + 1) * GLA_HEAD_DV)
        state = states[h]
        o_chunks = []
        for c in range(n_gla):
            rs = slice(c * GLA_CHUNK, (c + 1) * GLA_CHUNK)
            rhs = jnp.concatenate([vb[rs, vs], zero_v, state.astype(BF16)], axis=0)
            o_chunks.append(_dot(lhs[h, c], rhs))
            state = decay_t[ks, c:c + 1] * state + kv[h, c]
        new_states.append(state)
        o = jnp.concatenate(o_chunks, axis=0)
        on = o * lax.rsqrt(jnp.mean(o * o, axis=-1, keepdims=True) + EPS) * gnw_ref[...]
        yb_heads.append((on * rgate[:, vs]).astype(BF16))
    yb = jnp.concatenate(yb_heads, axis=1)

    bgt = bgt_ref[...]
    ga = _sigmoid(gates[:, :D_MODEL] + bgt[:, :D_MODEL]) * pa
    pb = _dot(yb, wb_ref[:, :D_MODEL])
    merged = ga + _sigmoid(gates[:, D_MODEL:] + bgt[:, D_MODEL:]) * pb
    return x + _dot(merged.astype(BF16), wo_ref[:, :D_MODEL]), new_states


def _mixer_kernel(x_ref, *refs):
    *param_refs, fg_ref, fu_ref, fd_ref, h_ref, fg_out_ref, fu_out_ref, fd_out_ref, state_ref = refs
    fg_out_ref[...] = fg_ref[...].astype(BF16)
    fu_out_ref[...] = fu_ref[...].astype(BF16)
    fd_out_ref[...] = fd_ref[...].astype(BF16)

    @pl.when(pl.program_id(1) == 0)
    def _():
        state_ref[...] = jnp.zeros_like(state_ref)

    sub = x_ref.shape[1] // MIXER_SUBTILES
    states = [state_ref[h] for h in range(GLA_HEADS)]
    for i in range(MIXER_SUBTILES):
        rows = slice(i * sub, (i + 1) * sub)
        h_rows, states = _mix_rows(x_ref[0, rows, :], states, *param_refs)
        h_ref[0, rows, :] = h_rows
    for h in range(GLA_HEADS):
        state_ref[h] = states[h]


def _ffn_kernel(h_ref, n2w_ref, wg_ref, wu_ref, wd_ref, fnw_ref, out_ref, *, final_norm):
    tile = h_ref.shape[0]
    sub = tile // FFN_SUBTILES
    groups = [slice(i * sub, (i + 1) * sub) for i in range(FFN_SUBTILES)]
    hs, gs, us = [], [], []
    for rows in groups:
        h = h_ref[rows, :]
        xn = _rms_norm(h, n2w_ref[...]).astype(BF16)
        hs.append(h)
        gs.append(_dot(xn, wg_ref[...]))
        us.append(_dot(xn, wu_ref[...]))
    for rows, h, g, u in zip(groups, hs, gs, us):
        a = (g * jax.nn.sigmoid(g) * u).astype(BF16)
        h2 = h + _dot(a, wd_ref[...])
        if final_norm:
            h2 = _rms_norm(h2, fnw_ref[...])
        out_ref[rows, :] = h2


def _resident(shape):
    nd = len(shape)
    return pl.BlockSpec(shape, lambda *_: (0,) * nd, pipeline_mode=pl.Buffered(1))


def _cast_spec(shape, n_outer, n_inner):
    rows, cols = shape
    n_steps = n_outer * n_inner
    n_blocks = max(nb for nb in range(1, n_steps + 1)
                   if n_steps % nb == 0 and rows % (nb * BF16_SUBLANES) == 0)
    per_block = n_steps // n_blocks
    return pl.BlockSpec((rows // n_blocks, cols),
                        lambda b, j: ((b * n_inner + j) // per_block, 0))


def _mixer(x, params, cast_weights):
    batch, seq, _ = x.shape
    tile = MIXER_TILE
    assert seq % tile == 0 and tile % (MIXER_SUBTILES * SGU_CHUNK) == 0
    assert SGU_CHUNK % GLA_CHUNK == 0
    n_tiles = seq // tile
    cast_specs = [_cast_spec(w.shape, batch, n_tiles) for w in cast_weights]
    in_specs = [pl.BlockSpec((1, tile, D_MODEL), lambda b, j: (b, j, 0))]
    in_specs += [_resident(p.shape) for p in params]
    in_specs += cast_specs
    return pl.pallas_call(
        _mixer_kernel,
        grid=(batch, n_tiles),
        in_specs=in_specs,
        out_specs=[pl.BlockSpec((1, tile, D_MODEL), lambda b, j: (b, j, 0))] + cast_specs,
        out_shape=[jax.ShapeDtypeStruct(x.shape, F32)]
        + [jax.ShapeDtypeStruct(w.shape, BF16) for w in cast_weights],
        scratch_shapes=[
            pltpu.VMEM((GLA_HEADS, GLA_HEAD_DK, GLA_HEAD_DV), F32),
        ],
        compiler_params=pltpu.CompilerParams(
            dimension_semantics=("arbitrary", "arbitrary"),
            vmem_limit_bytes=VMEM_LIMIT_BYTES),
        name="mixer",
    )(x, *params, *cast_weights)


def _ffn(h2d, params, final_norm):
    n_tok = h2d.shape[0]
    tile = FFN_TILE
    assert n_tok % tile == 0 and tile % (FFN_SUBTILES * SUBLANES) == 0
    in_specs = [pl.BlockSpec((tile, D_MODEL), lambda i: (i, 0))]
    in_specs += [_resident(p.shape) for p in params]
    return pl.pallas_call(
        functools.partial(_ffn_kernel, final_norm=final_norm),
        grid=(n_tok // tile,),
        in_specs=in_specs,
        out_specs=pl.BlockSpec((tile, D_MODEL), lambda i: (i, 0)),
        out_shape=jax.ShapeDtypeStruct(h2d.shape, F32),
        compiler_params=pltpu.CompilerParams(
            dimension_semantics=("arbitrary",),
            vmem_limit_bytes=VMEM_LIMIT_BYTES),
        name="ffn",
    )(h2d, *params)


def _wprep_kernel(uv_ref, qk_ref, vr0_ref, vr1_ref, g0_ref, g1_ref, g2_ref, wa_ref, wb_ref, wo_ref,
                  wuv_out, wqkl_out, wvr_out, wgt_out, wa_out, wb_out, wo_out):
    kb = uv_ref.shape[1]
    eye = (lax.broadcasted_iota(jnp.int32, (kb, kb), 0)
           == lax.broadcasted_iota(jnp.int32, (kb, kb), 1)).astype(BF16)

    def transposed(block):
        return lax.dot_general(eye, block.astype(BF16), (((1,), (1,)), ((), ())),
                               preferred_element_type=F32).astype(BF16)

    zero_tile = jnp.zeros((kb, LANES), BF16)

    def put(out_ref, body, tail):
        width = body.shape[1]
        out_ref[:, :width] = body
        out_ref[:, width:] = tail

    g0 = g0_ref[...]
    lane = lax.broadcasted_iota(jnp.int32, (kb, LANES), 1)
    lowrank = jnp.where(lane < GLA_LOWRANK, transposed(g0[:LANES]), 0)
    gate_rows = jnp.concatenate([g0[GLA_LOWRANK:], g1_ref[...], g2_ref[...]], axis=0)
    put(wuv_out, transposed(uv_ref[...]), zero_tile)
    put(wqkl_out, transposed(qk_ref[...]), lowrank)
    put(wvr_out, transposed(jnp.concatenate([vr0_ref[...], vr1_ref[...]], axis=0)), zero_tile)
    put(wgt_out, transposed(gate_rows), zero_tile)
    put(wa_out, wa_ref[...].astype(BF16), zero_tile)
    put(wb_out, wb_ref[...].astype(BF16), zero_tile)
    put(wo_out, wo_ref[...].astype(BF16), zero_tile)


def _wprep(w_in_t, w_a, w_b, w_o):
    kb = WPREP_ROWS
    assert D_MODEL % kb == 0
    half = D_MODEL
    assert O_QK == 2 * half and O_VR == 3 * half and O_LR == 5 * half
    tail = w_in_t.shape[0] - 7 * half
    assert tail == GLA_LOWRANK and (7 * half) % tail == 0

    def win(rows, block_index):
        return pl.BlockSpec((rows, kb), lambda i: (block_index, i))

    widths = [O_QK, O_VR - O_QK, O_LR - O_VR, 2 * D_MODEL, D_MODEL, D_MODEL, D_MODEL]
    row_block = lambda cols: pl.BlockSpec((kb, cols), lambda i: (i, 0))
    return pl.pallas_call(
        _wprep_kernel,
        grid=(D_MODEL // kb,),
        in_specs=[win(2 * half, 0), win(half, 2), win(half, 3), win(half, 4),
                  win(half, 5), win(half, 6), win(tail, 7 * half // tail)]
        + [row_block(D_MODEL)] * 3,
        out_specs=[row_block(n + LANES) for n in widths],
        out_shape=[jax.ShapeDtypeStruct((D_MODEL, n + LANES), BF16) for n in widths],
        compiler_params=pltpu.CompilerParams(
            dimension_semantics=("arbitrary",),
            vmem_limit_bytes=VMEM_LIMIT_BYTES),
        name="wprep",
    )(*([w_in_t] * 7), w_a, w_b, w_o)


def _row(v):
    return v.reshape(1, -1).astype(F32)


def kernel(x, norm1_w, w_in, b_in, sgu_ln_w, sgu_ln_b, sgu_w, sgu_b, w_gk2, b_gk, gla_norm_w,
           w_branch_a, w_branch_b, w_out, norm2_w, w_ffn_gate, w_ffn_up, w_ffn_down,
           final_norm_w):
    depth = w_in.shape[0]
    batch, seq, _ = x.shape
    h = x
    for l in range(depth):
        bl = b_in[l]
        lr_pad = LOWRANK_PAD - GLA_LOWRANK
        wuv, wqkl, wvr, wgt, wa, wb, wo = _wprep(w_in[l].T, w_branch_a[l], w_branch_b[l], w_out[l])
        mixer_params = [
            _row(norm1_w[l]),
            wuv, _row(bl[:O_QK]),
            wqkl, _row(jnp.concatenate([bl[O_QK:O_VR], bl[O_LR:O_GT],
                                        jnp.zeros((lr_pad,), bl.dtype)])),
            wvr, _row(bl[O_VR:O_LR]),
            wgt, _row(bl[O_GT:]),
            _row(sgu_ln_w[l]), _row(sgu_ln_b[l]),
            sgu_w[l].astype(F32),
            jnp.repeat(sgu_b[l].T.astype(F32), SGU_GROUP_DIM, axis=1),
            jnp.pad(w_gk2[l], ((0, lr_pad), (0, 0))).astype(BF16), _row(b_gk[l]),
            _row(gla_norm_w[l]),
            wa, wb, wo,
        ]
        h, wg, wu, wd = _mixer(h, mixer_params, [w_ffn_gate[l], w_ffn_up[l], w_ffn_down[l]])
        ffn_params = [_row(norm2_w[l]), wg, wu, wd, _row(final_norm_w)]
        h = _ffn(h.reshape(batch * seq, D_MODEL), ffn_params,
                 final_norm=(l == depth - 1)).reshape(batch, seq, D_MODEL)
    return h
```
